```python
import math
import jax, jax.numpy as jnp
from jax import lax
import numpy as np

D_MODEL = 1024
BATCH = 16
SEQ = 4096
DEPTH = 2

CTX_LEN = 256
GRID_W = 64
MIX_WIDTH = D_MODEL
GROUP_W = MIX_WIDTH // 4
N_SUB = 4
SUB_W = GROUP_W // N_SUB
CONV_W = 3
POOL_WINDOWS = (2, 4, 8, 16)
MLA_HEADS = 4
MLA_NOPE = GROUP_W // MLA_HEADS
MLA_ROPE = MLA_NOPE // 2
MLA_V = GROUP_W // MLA_HEADS
MLA_Q_RANK = GROUP_W
MLA_KV_RANK = GROUP_W // 2
MLA_SCALE = 1.0 / math.sqrt(MLA_NOPE + MLA_ROPE)
ROPE_BASE = 10000.0
D_FF = 4 * D_MODEL
EPS = 1e-6
Q_BLOCK = 128

OFF_FOURIER = 0
OFF_CONV = OFF_FOURIER + GROUP_W
OFF_POOL = OFF_CONV + 3 * GROUP_W
OFF_MLA_Q = OFF_POOL + GROUP_W
OFF_MLA_KV = OFF_MLA_Q + MLA_Q_RANK
OFF_MLA_KR = OFF_MLA_KV + MLA_KV_RANK
IN_COLS = OFF_MLA_KR + MLA_ROPE

kernel_name = "hybrid_fourier_conv_pool_mla_dit"


def rmsnorm(x, g):
    xf = x.astype(jnp.float32)
    y = xf * lax.rsqrt(jnp.mean(xf * xf, axis=-1, keepdims=True) + EPS)
    return (y * g.astype(jnp.float32)).astype(x.dtype)


def modulate(h, shift, scale):
    return h * (1.0 + scale) + shift


def axial_rope_tables(n):
    rows = n // GRID_W
    row = jnp.repeat(jnp.arange(rows, dtype=jnp.float32), GRID_W)
    col = jnp.tile(jnp.arange(GRID_W, dtype=jnp.float32), rows)
    half = MLA_ROPE // 2
    inv = ROPE_BASE ** (-jnp.arange(0, half, 2, dtype=jnp.float32) / half)
    ang_r = row[:, None] * inv[None, :]
    ang_c = col[:, None] * inv[None, :]
    ang = jnp.concatenate([ang_r, ang_r, ang_c, ang_c], axis=-1)
    return jnp.cos(ang), jnp.sin(ang)


def apply_axial_rope(x, cos, sin):
    half = MLA_ROPE // 2
    quarter = half // 2

    def rot(v):
        return jnp.concatenate([-v[..., quarter:], v[..., :quarter]], axis=-1)

    rotated = jnp.concatenate([rot(x[..., :half]), rot(x[..., half:])], axis=-1)
    return x * cos.astype(x.dtype) + rotated * sin.astype(x.dtype)


def fourier_mix(u, w):
    b, n, _ = u.shape
    f = jnp.fft.fft2(u.astype(jnp.float32).reshape(b, n, N_SUB, SUB_W), axes=(1, 3), norm="ortho").real
    return f.reshape(b, n, GROUP_W).astype(u.dtype) @ w


def short_conv_mix(u, conv_w):
    bg, cg, xin = jnp.split(u, 3, axis=-1)
    z = cg * xin
    n = z.shape[1]
    pad = CONV_W // 2
    zp = jnp.pad(z, ((0, 0), (pad, CONV_W - 1 - pad), (0, 0)))
    y = sum(zp[:, k:k + n] * conv_w[k] for k in range(CONV_W))
    return bg * y


def pool_mix(u, pool_w, pool_scale):
    b, n, _ = u.shape
    uf = u.astype(jnp.float32)
    cs = jnp.concatenate([jnp.zeros((b, 1, GROUP_W), jnp.float32), jnp.cumsum(uf, axis=1)], axis=1)
    t = jnp.arange(n)
    outs = []
    for g, w in enumerate(POOL_WINDOWS):
        lo = jnp.maximum(t - w // 2, 0)
        hi = jnp.minimum(t + w // 2 - 1, n - 1)
        sl = slice(g * SUB_W, (g + 1) * SUB_W)
        csg = cs[..., sl]
        s = jnp.take(csg, hi + 1, axis=1) - jnp.take(csg, lo, axis=1)
        cnt = (hi - lo + 1).astype(jnp.float32)[None, :, None]
        outs.append((s / cnt - uf[..., sl]).astype(u.dtype) @ pool_w[g])
    return jnp.concatenate(outs, axis=-1) * pool_scale


def mla_project(p, q_norm_g, w_uq, kv_norm_g, w_ukv):
    b, n, _ = p.shape
    cq = p[..., OFF_MLA_Q:OFF_MLA_KV]
    ckv = p[..., OFF_MLA_KV:OFF_MLA_KR]
    k_pe = p[..., OFF_MLA_KR:IN_COLS]
    q = (rmsnorm(cq, q_norm_g) @ w_uq).reshape(b, n, MLA_HEADS, MLA_NOPE + MLA_ROPE)
    kv = (rmsnorm(ckv, kv_norm_g) @ w_ukv).reshape(b, n, MLA_HEADS, MLA_NOPE + MLA_V)
    return q[..., :MLA_NOPE], q[..., MLA_NOPE:], kv[..., :MLA_NOPE], k_pe, kv[..., MLA_NOPE:]


def mla_attend(q_nope, q_pe, k_nope, k_pe, v):
    s = (jnp.einsum('bqhd,bkhd->bhqk', q_nope, k_nope, preferred_element_type=jnp.float32)
         + jnp.einsum('bqhr,bkr->bhqk', q_pe, k_pe, preferred_element_type=jnp.float32))
    p = jax.nn.softmax(s * MLA_SCALE, axis=-1)
    return jnp.einsum('bhqk,bkhd->bqhd', p.astype(v.dtype), v)


def blocked_mla_attend(q_nope, q_pe, k_nope, k_pe, v):
    b, n = q_nope.shape[:2]
    nb = n // Q_BLOCK
    qn = q_nope.reshape(b, nb, Q_BLOCK, MLA_HEADS, MLA_NOPE).transpose(1, 0, 2, 3, 4)
    qp = q_pe.reshape(b, nb, Q_BLOCK, MLA_HEADS, MLA_ROPE).transpose(1, 0, 2, 3, 4)
    out = lax.map(lambda qs: mla_attend(qs[0], qs[1], k_nope, k_pe, v), (qn, qp))
    return out.transpose(1, 0, 2, 3, 4).reshape(b, n, MLA_HEADS * MLA_V)


def local_mixers(p, fourier_w, conv_w, pool_w, pool_scale):
    return (fourier_mix(p[..., OFF_FOURIER:OFF_CONV], fourier_w),
            short_conv_mix(p[..., OFF_CONV:OFF_POOL], conv_w),
            pool_mix(p[..., OFF_POOL:OFF_MLA_Q], pool_w, pool_scale))


def sq_relu_mlp(h, w1, w2):
    return jnp.square(jax.nn.relu(h @ w1)) @ w2


def setup_inputs(seed: int = 0) -> dict:
    key = jax.random.key(seed)
    ks = jax.random.split(key, 24)
    f32 = jnp.float32
    nrm = lambda k, shape, s: jax.random.normal(k, shape, f32) * s
    L = DEPTH
    return {
        "x": nrm(ks[0], (BATCH, SEQ, D_MODEL), 1.0),
        "c": nrm(ks[1], (BATCH, D_MODEL), 1.0),
        "ctx": nrm(ks[2], (BATCH, CTX_LEN, D_MODEL), 1.0),
        "c_ctx": nrm(ks[3], (D_MODEL,), 1.0),
        "ada_w": nrm(ks[4], (L, D_MODEL, 6 * D_MODEL), 0.5 * D_MODEL ** -0.5),
        "ada_b": nrm(ks[5], (L, 6 * D_MODEL), 0.02),
        "norm1_g": 1.0 + nrm(ks[6], (L, D_MODEL), 0.05),
        "norm2_g": 1.0 + nrm(ks[7], (L, D_MODEL), 0.05),
        "w_in": nrm(ks[8], (L, D_MODEL, IN_COLS), D_MODEL ** -0.5),
        "fourier_w": nrm(ks[9], (L, GROUP_W, GROUP_W), GROUP_W ** -0.5),
        "conv_w": nrm(ks[10], (L, CONV_W, GROUP_W), CONV_W ** -0.5),
        "pool_w": nrm(ks[11], (L, N_SUB, SUB_W, SUB_W), SUB_W ** -0.5),
        "pool_scale": 1.0 + nrm(ks[12], (L, GROUP_W), 0.1),
        "q_norm_g": 1.0 + nrm(ks[13], (L, MLA_Q_RANK), 0.05),
        "w_uq": nrm(ks[14], (L, MLA_Q_RANK, MLA_HEADS * (MLA_NOPE + MLA_ROPE)), MLA_Q_RANK ** -0.5),
        "kv_norm_g": 1.0 + nrm(ks[15], (L, MLA_KV_RANK), 0.05),
        "w_ukv": nrm(ks[16], (L, MLA_KV_RANK, MLA_HEADS * (MLA_NOPE + MLA_V)), MLA_KV_RANK ** -0.5),
        "w_out": nrm(ks[17], (L, MIX_WIDTH, D_MODEL), MIX_WIDTH ** -0.5),
        "mlp_w1": nrm(ks[18], (L, D_MODEL, D_FF), D_MODEL ** -0.5),
        "mlp_w2": nrm(ks[19], (L, D_FF, D_MODEL), D_FF ** -0.5),
        "final_norm_g": 1.0 + nrm(ks[20], (D_MODEL,), 0.05),
    }


def reference(x, c, ctx, c_ctx, ada_w, ada_b, norm1_g, norm2_g, w_in, fourier_w, conv_w,
              pool_w, pool_scale, q_norm_g, w_uq, kv_norm_g, w_ukv, w_out, mlp_w1, mlp_w2,
              final_norm_g):
    b, n, _ = x.shape
    cos, sin = axial_rope_tables(n)
    for l in range(DEPTH):
        last = l == DEPTH - 1
        mod_x = (jax.nn.silu(c) @ ada_w[l] + ada_b[l])[:, None, :]
        mod_c = jax.nn.silu(c_ctx) @ ada_w[l] + ada_b[l]
        sh1x, sc1x, g1x, sh2x, sc2x, g2x = jnp.split(mod_x, 6, axis=-1)
        sh1c, sc1c, g1c, sh2c, sc2c, g2c = jnp.split(mod_c, 6, axis=-1)

        hx = modulate(rmsnorm(x, norm1_g[l]), sh1x, sc1x)
        hc = modulate(rmsnorm(ctx, norm1_g[l]), sh1c, sc1c)
        px = hx @ w_in[l]
        pc = hc @ w_in[l]

        qn_c, qp_c, kn_c, kp_c, v_c = mla_project(pc, q_norm_g[l], w_uq[l], kv_norm_g[l], w_ukv[l])
        qn_x, qp_x, kn_x, kp_x, v_x = mla_project(px, q_norm_g[l], w_uq[l], kv_norm_g[l], w_ukv[l])
        qp_x = apply_axial_rope(qp_x, cos[:, None, :], sin[:, None, :])
        kp_x = apply_axial_rope(kp_x, cos, sin)
        kn_all = jnp.concatenate([kn_c, kn_x], axis=1)
        kp_all = jnp.concatenate([kp_c, kp_x], axis=1)
        v_all = jnp.concatenate([v_c, v_x], axis=1)
        attn_x = blocked_mla_attend(qn_x, qp_x, kn_all, kp_all, v_all)

        f_x, s_x, p_x = local_mixers(px, fourier_w[l], conv_w[l], pool_w[l], pool_scale[l])
        out_x = jnp.concatenate([f_x, s_x, p_x, attn_x], axis=-1) @ w_out[l]
        x = x + g1x * out_x
        x = x + g2x * sq_relu_mlp(modulate(rmsnorm(x, norm2_g[l]), sh2x, sc2x), mlp_w1[l], mlp_w2[l])

        if not last:
            attn_c = mla_attend(qn_c, qp_c, kn_c, kp_c, v_c).reshape(b, -1, GROUP_W)
            f_c, s_c, p_c = local_mixers(pc, fourier_w[l], conv_w[l], pool_w[l], pool_scale[l])
            out_c = jnp.concatenate([f_c, s_c, p_c, attn_c], axis=-1) @ w_out[l]
            ctx = ctx + g1c * out_c
            ctx = ctx + g2c * sq_relu_mlp(modulate(rmsnorm(ctx, norm2_g[l]), sh2c, sc2c), mlp_w1[l], mlp_w2[l])
    return rmsnorm(x, final_norm_g)
```

```python
import functools
import math

import jax
import jax.numpy as jnp
from jax import lax
from jax.experimental import pallas as pl
from jax.experimental.pallas import tpu as pltpu

D_MODEL = 1024
DEPTH = 2
GRID_W = 64
GROUP_W = D_MODEL // 4
N_SUB = 4
SUB_W = GROUP_W // N_SUB
POOL_WINDOWS = (2, 4, 8, 16)
MLA_HEADS = 4
MLA_NOPE = GROUP_W // MLA_HEADS
MLA_ROPE = MLA_NOPE // 2
MLA_V = GROUP_W // MLA_HEADS
MLA_Q_RANK = GROUP_W
MLA_KV_RANK = GROUP_W // 2
MLA_SCALE = 1.0 / math.sqrt(MLA_NOPE + MLA_ROPE)
ROPE_BASE = 10000.0
D_FF = 4 * D_MODEL
EPS = 1e-6

OFF_CONV = GROUP_W
OFF_POOL = OFF_CONV + 3 * GROUP_W
OFF_MLA_Q = OFF_POOL + GROUP_W
OFF_MLA_KV = OFF_MLA_Q + MLA_Q_RANK
OFF_MLA_KR = OFF_MLA_KV + MLA_KV_RANK
IN_COLS = OFF_MLA_KR + MLA_ROPE

LANES = 128
SUBLANES = 8
HEAD_PAD = LANES
EXT_COLS = OFF_MLA_KR + 2 * LANES
DFT_ROWS = 64
HALO = SUBLANES
VMEM_LIMIT = 52 * 1024 * 1024
LOG2E = math.log2(math.e)

F32 = jnp.float32
BF16 = jnp.bfloat16


def _dot(a, b):
    return jnp.dot(a, b, preferred_element_type=F32)


def _rms(x):
    return x * lax.rsqrt(jnp.mean(x * x, axis=-1, keepdims=True) + EPS)


def _params(*sem):
    return pltpu.CompilerParams(dimension_semantics=sem, vmem_limit_bytes=VMEM_LIMIT)


def _const_spec(shape):
    nd = len(shape)
    return pl.BlockSpec(shape, lambda *_: (0,) * nd, pipeline_mode=pl.Buffered(1))


def _ada_kernel(cc_ref, w_ref, b_ref, o_ref):
    cc = cc_ref[...]
    s = cc * (1.0 / (1.0 + jnp.exp(-cc)))
    o_ref[0] = _dot(s.astype(BF16), w_ref[0].astype(BF16)) + b_ref[0]


def _ada(cc, ada_w, ada_b):
    depth, d, cols = ada_w.shape
    rows = cc.shape[0]
    tn = 1536
    return pl.pallas_call(
        _ada_kernel,
        out_shape=jax.ShapeDtypeStruct((depth, rows, cols), F32),
        grid=(depth, cols // tn),
        in_specs=[pl.BlockSpec((rows, d), lambda l, j: (0, 0)),
                  pl.BlockSpec((1, d, tn), lambda l, j: (l, 0, j)),
                  pl.BlockSpec((1, 1, tn), lambda l, j: (l, 0, j))],
        out_specs=pl.BlockSpec((1, rows, tn), lambda l, j: (l, 0, j)),
        compiler_params=_params("arbitrary", "arbitrary"),
        name="ada",
    )(cc, ada_w, ada_b.reshape(depth, 1, cols))


def _fold_kernel(dft_ref, w_ref, o_ref):
    w = w_ref[0]
    c = jnp.dot(dft_ref[:, :GROUP_W], w, preferred_element_type=F32, precision=lax.Precision.HIGHEST)
    s = jnp.dot(dft_ref[:, GROUP_W:], w, preferred_element_type=F32, precision=lax.Precision.HIGHEST)
    o_ref[0, :, :GROUP_W] = c.astype(BF16)
    o_ref[0, :, GROUP_W:] = s.astype(BF16)


def _channel_dft():
    i = jnp.arange(GROUP_W, dtype=jnp.int32)
    same = (i[:, None] // SUB_W) == (i[None, :] // SUB_W)
    ang = ((i[:, None] % SUB_W) * (i[None, :] % SUB_W) % SUB_W).astype(F32) * (2.0 * math.pi / SUB_W)
    scale = 1.0 / math.sqrt(SUB_W)
    c = jnp.where(same, jnp.cos(ang), 0.0) * scale
    s = jnp.where(same, jnp.sin(ang), 0.0) * scale
    return jnp.concatenate([c, s], axis=1)


def _fold_fourier(fourier_w):
    depth = fourier_w.shape[0]
    return pl.pallas_call(
        _fold_kernel,
        out_shape=jax.ShapeDtypeStruct((depth, GROUP_W, 2 * GROUP_W), BF16),
        grid=(depth,),
        in_specs=[pl.BlockSpec((GROUP_W, 2 * GROUP_W), lambda l: (0, 0)),
                  pl.BlockSpec((1, GROUP_W, GROUP_W), lambda l: (l, 0, 0))],
        out_specs=pl.BlockSpec((1, GROUP_W, 2 * GROUP_W), lambda l: (l, 0, 0)),
        compiler_params=_params("arbitrary"),
        name="fold_fourier",
    )(_channel_dft(), fourier_w)


def _dftgen_kernel(n, c1_ref, s1_ref, cb_ref, sb_ref, o_ref):
    c1 = c1_ref[0]
    s1 = s1_ref[0]
    cb = cb_ref[...]
    sb = sb_ref[...]
    scale = 1.0 / math.sqrt(n)
    o_ref[:, :n] = ((c1 * cb - s1 * sb) * scale).astype(BF16)
    o_ref[:, n:] = ((s1 * cb + c1 * sb) * (-scale)).astype(BF16)


def _dft_matrix(n):
    nb = n // DFT_ROWS
    k = jnp.arange(n, dtype=jnp.int32)
    j1 = jnp.arange(nb, dtype=jnp.int32) * DFT_ROWS
    j0 = jnp.arange(DFT_ROWS, dtype=jnp.int32)
    w = 2.0 * math.pi / n
    a1 = ((j1[:, None] * k[None, :]) % n).astype(F32) * w
    a0 = ((j0[:, None] * k[None, :]) % n).astype(F32) * w
    row_spec = pl.BlockSpec((1, 1, n), lambda i: (i, 0, 0))
    tab_spec = pl.BlockSpec((DFT_ROWS, n), lambda i: (0, 0))
    return pl.pallas_call(
        functools.partial(_dftgen_kernel, n),
        out_shape=jax.ShapeDtypeStruct((n, 2 * n), BF16),
        grid=(nb,),
        in_specs=[row_spec, row_spec, tab_spec, tab_spec],
        out_specs=pl.BlockSpec((DFT_ROWS, 2 * n), lambda i: (i, 0)),
        compiler_params=_params("arbitrary"),
        name="dft_matrix",
    )(jnp.cos(a1).reshape(nb, 1, n), jnp.sin(a1).reshape(nb, 1, n), jnp.cos(a0), jnp.sin(a0))


def _matmul_kernel(a_ref, b_ref, o_ref, acc_ref):
    k = pl.program_id(2)

    @pl.when(k == 0)
    def _():
        acc_ref[...] = jnp.zeros_like(acc_ref)

    acc_ref[...] += _dot(a_ref[...], b_ref[...])

    @pl.when(k == pl.num_programs(2) - 1)
    def _():
        o_ref[...] = acc_ref[...].astype(o_ref.dtype)


def _matmul(a, b, tm, tn, tk):
    m, kk = a.shape
    n = b.shape[1]
    return pl.pallas_call(
        _matmul_kernel,
        out_shape=jax.ShapeDtypeStruct((m, n), BF16),
        grid=(m // tm, n // tn, kk // tk),
        in_specs=[pl.BlockSpec((tm, tk), lambda i, j, k: (i, k)),
                  pl.BlockSpec((tk, tn), lambda i, j, k: (k, j))],
        out_specs=pl.BlockSpec((tm, tn), lambda i, j, k: (i, j)),
        scratch_shapes=[pltpu.VMEM((tm, tn), F32)],
        compiler_params=_params("parallel", "parallel", "arbitrary"),
        name="dft_matmul",
    )(a, b)


def _proj_kernel(x_ref, sc_ref, sh_ref, g_ref, win_ref, cw_ref, qg_ref, wq_ref, kvg_ref, wkv_ref,
                 cosq_ref, sinq_ref, cosk_ref, sink_ref, ab_ref, cp_ref, q_ref, k_ref, v_ref):
    x = x_ref[0]
    h = _rms(x) * (g_ref[...] * (1.0 + sc_ref[0])) + sh_ref[0]
    hb = h.astype(BF16)

    u = _dot(hb, win_ref[:, :OFF_CONV])
    ab = _dot(u.astype(BF16), cw_ref[...])
    ab_ref[0] = ab[:, :GROUP_W].astype(BF16)
    ab_ref[1] = ab[:, GROUP_W:].astype(BF16)

    cp_ref[0] = _dot(hb, win_ref[:, OFF_CONV:OFF_MLA_Q])

    cq = _dot(hb, win_ref[:, OFF_MLA_Q:OFF_MLA_KV])
    cqn = (_rms(cq) * qg_ref[...]).astype(BF16)
    qq = _dot(cqn, wq_ref[...])
    cosq = cosq_ref[...]
    sinq = sinq_ref[...]
    half = MLA_HEADS * HEAD_PAD
    for hd in range(MLA_HEADS):
        lo = hd * HEAD_PAD
        q_ref[0, :, lo:lo + HEAD_PAD] = (qq[:, lo:lo + HEAD_PAD] * cosq
                                          + qq[:, half + lo:half + lo + HEAD_PAD] * sinq).astype(BF16)

    ckv = _dot(hb, win_ref[:, OFF_MLA_KV:OFF_MLA_KR])
    ckvn = (_rms(ckv) * kvg_ref[...]).astype(BF16)
    kv = _dot(ckvn, wkv_ref[...])
    kp2 = _dot(hb, win_ref[:, OFF_MLA_KR:EXT_COLS])
    kp = kp2[:, :LANES] * cosk_ref[...] + kp2[:, LANES:] * sink_ref[...]
    for hd in range(MLA_HEADS):
        lo = hd * HEAD_PAD
        k_ref[0, :, lo:lo + HEAD_PAD] = (kv[:, lo:lo + HEAD_PAD] + kp).astype(BF16)
    v_ref[0] = kv[:, half:].astype(BF16)


def _proj(x, scale, shift, g, win, cw, qg, wq, kvg, wkv, cosq, sinq, cosk, sink, t):
    b, n, d = x.shape
    row = lambda bi, i: (bi, 0, 0)
    tile = lambda bi, i: (bi, i, 0)
    tab = pl.BlockSpec((t, LANES), lambda bi, i: (i, 0))
    qk_w = MLA_HEADS * HEAD_PAD
    return pl.pallas_call(
        _proj_kernel,
        out_shape=(jax.ShapeDtypeStruct((2, n, b * GROUP_W), BF16),
                   jax.ShapeDtypeStruct((b, n, 4 * GROUP_W), F32),
                   jax.ShapeDtypeStruct((b, n, qk_w), BF16),
                   jax.ShapeDtypeStruct((b, n, qk_w), BF16),
                   jax.ShapeDtypeStruct((b, n, GROUP_W), BF16)),
        grid=(b, n // t),
        in_specs=[pl.BlockSpec((1, t, d), tile),
                  pl.BlockSpec((1, 1, d), row),
                  pl.BlockSpec((1, 1, d), row),
                  _const_spec(g.shape), _const_spec(win.shape), _const_spec(cw.shape),
                  _const_spec(qg.shape), _const_spec(wq.shape), _const_spec(kvg.shape),
                  _const_spec(wkv.shape), tab, tab, tab, tab],
        out_specs=(pl.BlockSpec((2, t, GROUP_W), lambda bi, i: (0, i, bi)),
                   pl.BlockSpec((1, t, 4 * GROUP_W), tile),
                   pl.BlockSpec((1, t, qk_w), tile),
                   pl.BlockSpec((1, t, qk_w), tile),
                   pl.BlockSpec((1, t, GROUP_W), tile)),
        compiler_params=_params("parallel", "parallel"),
        name="proj",
    )(x, scale, shift, g, win, cw, qg, wq, kvg, wkv, cosq, sinq, cosk, sink)


def _attn_kernel(q_ref, k_ref, v_ref, o_ref):
    lane = lax.broadcasted_iota(jnp.int32, (q_ref.shape[1], LANES), 1)
    outs = []
    for hd in range(MLA_HEADS):
        lo = hd * HEAD_PAD
        qh = q_ref[0, :, lo:lo + HEAD_PAD]
        kh = k_ref[0, :, lo:lo + HEAD_PAD]
        s = lax.dot_general(qh, kh, (((1,), (1,)), ((), ())), preferred_element_type=F32)
        m = jnp.max(s, axis=-1, keepdims=True)
        p = jnp.exp2(s - m)
        l = jnp.sum(p, axis=-1, keepdims=True)
        pair = (hd // 2) * LANES
        o = _dot(p.astype(BF16), v_ref[0, :, pair:pair + LANES])
        outs.append(o / l)
    lo_half = lane < MLA_V
    o_ref[0, :, :LANES] = jnp.where(lo_half, outs[0], outs[1]).astype(o_ref.dtype)
    o_ref[0, :, LANES:] = jnp.where(lo_half, outs[2], outs[3]).astype(o_ref.dtype)


def _attn(q, k, v, tq):
    b, n, qk_w = q.shape
    nk = k.shape[1]
    return pl.pallas_call(
        _attn_kernel,
        out_shape=jax.ShapeDtypeStruct((b, n, GROUP_W), BF16),
        grid=(b, n // tq),
        in_specs=[pl.BlockSpec((1, tq, qk_w), lambda bi, i: (bi, i, 0)),
                  pl.BlockSpec((1, nk, qk_w), lambda bi, i: (bi, 0, 0)),
                  pl.BlockSpec((1, nk, GROUP_W), lambda bi, i: (bi, 0, 0))],
        out_specs=pl.BlockSpec((1, tq, GROUP_W), lambda bi, i: (bi, i, 0)),
        compiler_params=_params("parallel", "arbitrary"),
        name="attn",
    )(q, k, v)


def _mixmlp_kernel(n_tokens, final, x_ref, cp_ref, prev_ref, next_ref, f_ref, at_ref, g1_ref, sc2_ref,
                   sh2_ref, g2_ref, convw_ref, poolw_ref, pscale_ref, wout_ref, n2g_ref, w1_ref, w2_ref,
                   fg_ref, o_ref):
    i = pl.program_id(1)
    t = x_ref.shape[1]
    ext = t + 2 * HALO
    first = i == 0
    last = i == pl.num_programs(1) - 1
    prev = jnp.where(first, 0.0, prev_ref[0])
    nxt = jnp.where(last, 0.0, next_ref[0])
    cpe = jnp.concatenate([prev, cp_ref[0], nxt], axis=0)

    def shifted(a, k):
        return pltpu.roll(a, k % ext, axis=0)

    z = cpe[:, GROUP_W:2 * GROUP_W] * cpe[:, 2 * GROUP_W:3 * GROUP_W]
    y = (shifted(z, 1) * convw_ref[0:1, :] + z * convw_ref[1:2, :] + shifted(z, -1) * convw_ref[2:3, :])
    conv = (cpe[:, :GROUP_W] * y)[HALO:HALO + t]

    u = cpe[:, 3 * GROUP_W:]
    s2 = shifted(u, 1) + u
    s4 = shifted(s2, 1) + shifted(s2, -1)
    ub = u[:, LANES:]
    s4b = s4[:, LANES:]
    s8b = shifted(s4b, 2) + shifted(s4b, -2)
    s16b = shifted(s8b, 4) + shifted(s8b, -4)
    lane = lax.broadcasted_iota(jnp.int32, (t, LANES), 1)
    tok = lax.broadcasted_iota(jnp.int32, (t, LANES), 0) + i * t
    lo_half = lane < SUB_W

    def mean(sum_lo, sum_hi, w_lo, w_hi):
        wh = jnp.where(lo_half, w_lo // 2, w_hi // 2)
        cnt = jnp.minimum(tok + wh - 1, n_tokens - 1) - jnp.maximum(tok - wh, 0) + 1
        s = jnp.where(lo_half, sum_lo[HALO:HALO + t], sum_hi[HALO:HALO + t])
        return s / cnt.astype(F32)

    pa = mean(s2[:, :LANES], s4[:, :LANES], POOL_WINDOWS[0], POOL_WINDOWS[1]) - u[HALO:HALO + t, :LANES]
    pb = mean(s8b, s16b, POOL_WINDOWS[2], POOL_WINDOWS[3]) - ub[HALO:HALO + t]
    pin = jnp.concatenate([pa, pb], axis=-1).astype(BF16)
    pool = _dot(pin, poolw_ref[...]) * pscale_ref[...]

    mix = jnp.concatenate([f_ref[...], conv.astype(BF16), pool.astype(BF16), at_ref[0]], axis=-1)
    x1 = x_ref[0] + g1_ref[0] * _dot(mix, wout_ref[...])

    h2 = (_rms(x1) * (n2g_ref[...] * (1.0 + sc2_ref[0])) + sh2_ref[0]).astype(BF16)
    acc = None
    chunk = D_FF // 4
    for c in range(4):
        hid = jnp.maximum(_dot(h2, w1_ref[:, c * chunk:(c + 1) * chunk]), 0.0)
        part = _dot((hid * hid).astype(BF16), w2_ref[c * chunk:(c + 1) * chunk, :])
        acc = part if acc is None else acc + part
    x2 = x1 + g2_ref[0] * acc
    if final:
        x2 = _rms(x2) * fg_ref[...]
    o_ref[0] = x2


def _mixmlp(x, cp, f, at, g1, sc2, sh2, g2, convw, poolw, pscale, wout, n2g, w1, w2, fg, t, final):
    b, n, d = x.shape
    hb = t // HALO
    nhb = n // HALO
    row = lambda bi, i: (bi, 0, 0)
    tile = lambda bi, i: (bi, i, 0)
    row_spec = pl.BlockSpec((1, 1, d), row)
    return pl.pallas_call(
        functools.partial(_mixmlp_kernel, n, final),
        out_shape=jax.ShapeDtypeStruct((b, n, d), F32),
        grid=(b, n // t),
        in_specs=[pl.BlockSpec((1, t, d), tile),
                  pl.BlockSpec((1, t, 4 * GROUP_W), tile),
                  pl.BlockSpec((1, HALO, 4 * GROUP_W), lambda bi, i: (bi, jnp.maximum(i * hb - 1, 0), 0)),
                  pl.BlockSpec((1, HALO, 4 * GROUP_W), lambda bi, i: (bi, jnp.minimum((i + 1) * hb, nhb - 1), 0)),
                  pl.BlockSpec((t, GROUP_W), lambda bi, i: (i, bi)),
                  pl.BlockSpec((1, t, GROUP_W), tile),
                  row_spec, row_spec, row_spec, row_spec,
                  _const_spec(convw.shape), _const_spec(poolw.shape), _const_spec(pscale.shape),
                  _const_spec(wout.shape), _const_spec(n2g.shape), _const_spec(w1.shape),
                  _const_spec(w2.shape), _const_spec(fg.shape)],
        out_specs=pl.BlockSpec((1, t, d), tile),
        compiler_params=_params("parallel", "arbitrary"),
        name="mixmlp",
    )(x, cp, cp, cp, f, at, g1, sc2, sh2, g2, convw, poolw, pscale, wout, n2g, w1, w2, fg)


def _rope_rotate_cols(w):
    q = MLA_ROPE // 4
    return jnp.concatenate([-w[..., q:2 * q], w[..., :q], -w[..., 3 * q:], w[..., 2 * q:3 * q]], axis=-1)


def _layout_weights(w_in, w_uq, w_ukv, pool_w):
    d = w_in.shape[0]
    kr = w_in[:, OFF_MLA_KR:]
    z = lambda r, c: jnp.zeros((r, c), F32)
    pad_lo, pad_hi = MLA_NOPE, HEAD_PAD - MLA_NOPE - MLA_ROPE
    win = jnp.concatenate([w_in[:, :OFF_MLA_KR],
                           z(d, pad_lo), kr, z(d, pad_hi),
                           z(d, pad_lo), _rope_rotate_cols(kr), z(d, pad_hi)], axis=1)
    qh = w_uq.reshape(MLA_Q_RANK, MLA_HEADS, MLA_NOPE + MLA_ROPE)
    q_plain = jnp.pad(qh, ((0, 0), (0, 0), (0, pad_hi)))
    q_rot = jnp.pad(_rope_rotate_cols(qh[..., MLA_NOPE:]), ((0, 0), (0, 0), (pad_lo, pad_hi)))
    wq = jnp.concatenate([q_plain.reshape(MLA_Q_RANK, -1), q_rot.reshape(MLA_Q_RANK, -1)], axis=1)
    kvh = w_ukv.reshape(MLA_KV_RANK, MLA_HEADS, MLA_NOPE + MLA_V)
    k_plain = jnp.pad(kvh[..., :MLA_NOPE], ((0, 0), (0, 0), (0, HEAD_PAD - MLA_NOPE)))
    wkv = jnp.concatenate([k_plain.reshape(MLA_KV_RANK, -1), kvh[..., MLA_NOPE:].reshape(MLA_KV_RANK, -1)], axis=1)
    poolw = jax.scipy.linalg.block_diag(*[pool_w[g] for g in range(N_SUB)])
    return win.astype(BF16), wq.astype(BF16), wkv.astype(BF16), poolw.astype(BF16)


def _rope_tables(n):
    rows = n // GRID_W
    row = jnp.repeat(jnp.arange(rows, dtype=F32), GRID_W)
    col = jnp.tile(jnp.arange(GRID_W, dtype=F32), rows)
    half = MLA_ROPE // 2
    inv = ROPE_BASE ** (-jnp.arange(0, half, 2, dtype=F32) / half)
    ang_r = row[:, None] * inv[None, :]
    ang_c = col[:, None] * inv[None, :]
    ang = jnp.concatenate([ang_r, ang_r, ang_c, ang_c], axis=-1)
    return _place_tables(jnp.cos(ang), jnp.sin(ang))


def _place_tables(cos, sin):
    n = cos.shape[0]
    qs = MLA_SCALE * LOG2E
    pad_hi = HEAD_PAD - MLA_NOPE - MLA_ROPE
    zl, zh = jnp.zeros((n, MLA_NOPE), F32), jnp.zeros((n, pad_hi), F32)
    cosq = jnp.concatenate([jnp.full((n, MLA_NOPE), qs, F32), cos * qs, zh], axis=1)
    sinq = jnp.concatenate([zl, sin * qs, zh], axis=1)
    cosk = jnp.concatenate([zl, cos, zh], axis=1)
    sink = jnp.concatenate([zl, sin, zh], axis=1)
    return cosq, sinq, cosk, sink


def _tile(n, want):
    return min(n, want)


def kernel(x, c, ctx, c_ctx, ada_w, ada_b, norm1_g, norm2_g, w_in, fourier_w, conv_w, pool_w, pool_scale,
           q_norm_g, w_uq, kv_norm_g, w_ukv, w_out, mlp_w1, mlp_w2, final_norm_g):
    b, n, d = x.shape
    nc = ctx.shape[1]
    depth = ada_w.shape[0]

    rows = -(-(b + 1) // SUBLANES) * SUBLANES
    cc = jnp.concatenate([c, c_ctx[None, :], jnp.zeros((rows - b - 1, d), F32)], axis=0)
    mod = _ada(cc, ada_w, ada_b)
    cw = _fold_fourier(fourier_w)
    g_x = _dft_matrix(n)
    g_c = _dft_matrix(nc)
    tabs_x = _rope_tables(n)
    tabs_c = _place_tables(jnp.ones((nc, MLA_ROPE), F32), jnp.zeros((nc, MLA_ROPE), F32))
    fg = final_norm_g.reshape(1, d)

    def stream_mod(l, ctx_stream):
        m = mod[l, b:b + 1] if ctx_stream else mod[l, :b]
        m = jnp.broadcast_to(m, (b, 6 * d)).reshape(b, 1, 6, d)
        return [m[:, :, j, :] for j in range(6)]

    for l in range(depth):
        last = l == depth - 1
        win, wq, wkv, poolw = _layout_weights(w_in[l], w_uq[l], w_ukv[l], pool_w[l])
        n1g = norm1_g[l].reshape(1, d)
        n2g = norm2_g[l].reshape(1, d)
        qg = q_norm_g[l].reshape(1, -1)
        kvg = kv_norm_g[l].reshape(1, -1)
        pscale = pool_scale[l].reshape(1, -1)
        wout = w_out[l].astype(BF16)
        w1 = mlp_w1[l].astype(BF16)
        w2 = mlp_w2[l].astype(BF16)

        def project(tokens, ctx_stream, tabs):
            sh1, sc1 = stream_mod(l, ctx_stream)[:2]
            t = _tile(tokens.shape[1], 512)
            return _proj(tokens, sc1, sh1, n1g, win, cw[l], qg, wq, kvg, wkv, *tabs, t)

        def finish(tokens, ctx_stream, cp, ab, g, q, k, v, final):
            nt = tokens.shape[1]
            _, _, g1, sh2, sc2, g2 = stream_mod(l, ctx_stream)
            f = _matmul(g, ab.reshape(2 * nt, b * GROUP_W), _tile(nt, 1024), _tile(b * GROUP_W, 2048),
                        _tile(2 * nt, 1024))
            at = _attn(q, k, v, _tile(nt, 256))
            return _mixmlp(tokens, cp, f, at, g1, sc2, sh2, g2, conv_w[l], poolw, pscale, wout, n2g, w1, w2,
                           fg, _tile(nt, 512), final)

        ab_c, cp_c, q_c, k_c, v_c = project(ctx, True, tabs_c)
        ab_x, cp_x, q_x, k_x, v_x = project(x, False, tabs_x)
        k_all = jnp.concatenate([k_c, k_x], axis=1)
        v_all = jnp.concatenate([v_c, v_x], axis=1)
        x = finish(x, False, cp_x, ab_x, g_x, q_x, k_all, v_all, last)
        if not last:
            ctx = finish(ctx, True, cp_c, ab_c, g_c, q_c, k_c, v_c, False)
    return x
```

```python
import functools
import math

import jax
import jax.numpy as jnp
from jax import lax
from jax.experimental import pallas as pl
from jax.experimental.pallas import tpu as pltpu

D_MODEL = 1024
DEPTH = 2
GRID_W = 64
GROUP_W = D_MODEL // 4
N_SUB = 4
SUB_W = GROUP_W // N_SUB
POOL_WINDOWS = (2, 4, 8, 16)
MLA_HEADS = 4
MLA_NOPE = GROUP_W // MLA_HEADS
MLA_ROPE = MLA_NOPE // 2
MLA_V = GROUP_W // MLA_HEADS
MLA_Q_RANK = GROUP_W
MLA_KV_RANK = GROUP_W // 2
MLA_SCALE = 1.0 / math.sqrt(MLA_NOPE + MLA_ROPE)
ROPE_BASE = 10000.0
D_FF = 4 * D_MODEL
EPS = 1e-6

OFF_CONV = GROUP_W
OFF_POOL = OFF_CONV + 3 * GROUP_W
OFF_MLA_Q = OFF_POOL + GROUP_W
OFF_MLA_KV = OFF_MLA_Q + MLA_Q_RANK
OFF_MLA_KR = OFF_MLA_KV + MLA_KV_RANK
IN_COLS = OFF_MLA_KR + MLA_ROPE

LANES = 128
SUBLANES = 8
HEAD_PAD = LANES
EXT_COLS = OFF_MLA_KR + 2 * LANES
DFT_ROWS = 64
HALO = SUBLANES
MLP_CHUNKS = 4
PROJ_SUB_ROWS = 256
VMEM_LIMIT = 52 * 1024 * 1024
LOG2E = math.log2(math.e)

F32 = jnp.float32
BF16 = jnp.bfloat16


_NT = (((1,), (1,)), ((), ()))


def _dot(a, b):
    return jnp.dot(a, b, preferred_element_type=F32)


def _rms(x):
    return x * lax.rsqrt(jnp.mean(x * x, axis=-1, keepdims=True) + EPS)


def _params(*sem):
    return pltpu.CompilerParams(dimension_semantics=sem, vmem_limit_bytes=VMEM_LIMIT)


def _const_spec(shape):
    nd = len(shape)
    return pl.BlockSpec(shape, lambda *_: (0,) * nd, pipeline_mode=pl.Buffered(1))


def _ada_kernel(cc_ref, w_ref, b_ref, o_ref):
    cc = cc_ref[...]
    s = cc * (1.0 / (1.0 + jnp.exp(-cc)))
    o_ref[0] = _dot(s.astype(BF16), w_ref[0].astype(BF16)) + b_ref[0]


def _ada(cc, ada_w, ada_b):
    depth, d, cols = ada_w.shape
    rows = cc.shape[0]
    tn = 1536
    return pl.pallas_call(
        _ada_kernel,
        out_shape=jax.ShapeDtypeStruct((depth, rows, cols), F32),
        grid=(depth, cols // tn),
        in_specs=[pl.BlockSpec((rows, d), lambda l, j: (0, 0)),
                  pl.BlockSpec((1, d, tn), lambda l, j: (l, 0, j)),
                  pl.BlockSpec((1, 1, tn), lambda l, j: (l, 0, j))],
        out_specs=pl.BlockSpec((1, rows, tn), lambda l, j: (l, 0, j)),
        compiler_params=_params("arbitrary", "arbitrary"),
        name="ada",
    )(cc, ada_w, ada_b.reshape(depth, 1, cols))


def _fold_kernel(dft_ref, w_ref, o_ref):
    w = w_ref[0]
    c = jnp.dot(dft_ref[:, :GROUP_W], w, preferred_element_type=F32, precision=lax.Precision.HIGHEST)
    s = jnp.dot(dft_ref[:, GROUP_W:], w, preferred_element_type=F32, precision=lax.Precision.HIGHEST)
    o_ref[0, :, :GROUP_W] = c.astype(BF16)
    o_ref[0, :, GROUP_W:] = s.astype(BF16)


def _channel_dft():
    i = jnp.arange(GROUP_W, dtype=jnp.int32)
    same = (i[:, None] // SUB_W) == (i[None, :] // SUB_W)
    ang = ((i[:, None] % SUB_W) * (i[None, :] % SUB_W) % SUB_W).astype(F32) * (2.0 * math.pi / SUB_W)
    scale = 1.0 / math.sqrt(SUB_W)
    c = jnp.where(same, jnp.cos(ang), 0.0) * scale
    s = jnp.where(same, jnp.sin(ang), 0.0) * scale
    return jnp.concatenate([c, s], axis=1)


def _fold_fourier(fourier_w):
    depth = fourier_w.shape[0]
    return pl.pallas_call(
        _fold_kernel,
        out_shape=jax.ShapeDtypeStruct((depth, GROUP_W, 2 * GROUP_W), BF16),
        grid=(depth,),
        in_specs=[pl.BlockSpec((GROUP_W, 2 * GROUP_W), lambda l: (0, 0)),
                  pl.BlockSpec((1, GROUP_W, GROUP_W), lambda l: (l, 0, 0))],
        out_specs=pl.BlockSpec((1, GROUP_W, 2 * GROUP_W), lambda l: (l, 0, 0)),
        compiler_params=_params("arbitrary"),
        name="fold_fourier",
    )(_channel_dft(), fourier_w)


def _dftgen_kernel(n, c1_ref, s1_ref, cb_ref, sb_ref, o_ref):
    c1 = c1_ref[0]
    s1 = s1_ref[0]
    cb = cb_ref[...]
    sb = sb_ref[...]
    scale = 1.0 / math.sqrt(n)
    o_ref[:, :n] = ((c1 * cb - s1 * sb) * scale).astype(BF16)
    o_ref[:, n:] = ((s1 * cb + c1 * sb) * (-scale)).astype(BF16)


def _dft_matrix(n):
    nb = n // DFT_ROWS
    k = jnp.arange(n, dtype=jnp.int32)
    j1 = jnp.arange(nb, dtype=jnp.int32) * DFT_ROWS
    j0 = jnp.arange(DFT_ROWS, dtype=jnp.int32)
    w = 2.0 * math.pi / n
    a1 = ((j1[:, None] * k[None, :]) % n).astype(F32) * w
    a0 = ((j0[:, None] * k[None, :]) % n).astype(F32) * w
    row_spec = pl.BlockSpec((1, 1, n), lambda i: (i, 0, 0))
    tab_spec = pl.BlockSpec((DFT_ROWS, n), lambda i: (0, 0))
    return pl.pallas_call(
        functools.partial(_dftgen_kernel, n),
        out_shape=jax.ShapeDtypeStruct((n, 2 * n), BF16),
        grid=(nb,),
        in_specs=[row_spec, row_spec, tab_spec, tab_spec],
        out_specs=pl.BlockSpec((DFT_ROWS, 2 * n), lambda i: (i, 0)),
        compiler_params=_params("arbitrary"),
        name="dft_matrix",
    )(jnp.cos(a1).reshape(nb, 1, n), jnp.sin(a1).reshape(nb, 1, n), jnp.cos(a0), jnp.sin(a0))


def _matmul_kernel(a_ref, b_ref, o_ref, acc_ref):
    k = pl.program_id(2)

    @pl.when(k == 0)
    def _():
        acc_ref[...] = jnp.zeros_like(acc_ref)

    acc_ref[...] += _dot(a_ref[...], b_ref[...])

    @pl.when(k == pl.num_programs(2) - 1)
    def _():
        o_ref[...] = acc_ref[...].astype(o_ref.dtype)


def _matmul(a, b, tm, tn, tk):
    m, kk = a.shape
    n = b.shape[1]
    return pl.pallas_call(
        _matmul_kernel,
        out_shape=jax.ShapeDtypeStruct((m, n), BF16),
        grid=(m // tm, n // tn, kk // tk),
        in_specs=[pl.BlockSpec((tm, tk), lambda i, j, k: (i, k)),
                  pl.BlockSpec((tk, tn), lambda i, j, k: (k, j))],
        out_specs=pl.BlockSpec((tm, tn), lambda i, j, k: (i, j)),
        scratch_shapes=[pltpu.VMEM((tm, tn), F32)],
        compiler_params=_params("parallel", "parallel", "arbitrary"),
        name="dft_matmul",
    )(a, b)


def _proj_kernel(x_ref, sc_ref, sh_ref, g_ref, win_ref, cw_ref, qg_ref, wq_ref, kvg_ref, wk_ref, wvt_ref,
                 cosq_ref, sinq_ref, cosk_ref, sink_ref, ab_ref, cp_ref, q_ref, k_ref, vt_ref):
    gain = g_ref[...] * (1.0 + sc_ref[0])
    shift = sh_ref[0]
    half = MLA_HEADS * HEAD_PAD
    t = x_ref.shape[1]
    sub = min(t, PROJ_SUB_ROWS)
    for r0 in range(0, t, sub):
        rows = slice(r0, r0 + sub)
        hb = (_rms(x_ref[0, rows]) * gain + shift).astype(BF16)

        cq = _dot(hb, win_ref[:, OFF_MLA_Q:OFF_MLA_KV])
        ckv = _dot(hb, win_ref[:, OFF_MLA_KV:OFF_MLA_KR])
        u = _dot(hb, win_ref[:, :OFF_CONV])
        kp2 = _dot(hb, win_ref[:, OFF_MLA_KR:EXT_COLS])
        cp_ref[0, rows] = _dot(hb, win_ref[:, OFF_CONV:OFF_MLA_Q])

        cqn = (_rms(cq) * qg_ref[...]).astype(BF16)
        qq = _dot(cqn, wq_ref[...])
        cosq = cosq_ref[rows]
        sinq = sinq_ref[rows]
        for hd in range(MLA_HEADS):
            lo = hd * HEAD_PAD
            q_ref[0, rows, lo:lo + HEAD_PAD] = (qq[:, lo:lo + HEAD_PAD] * cosq
                                                 + qq[:, half + lo:half + lo + HEAD_PAD] * sinq).astype(BF16)

        ab = _dot(u.astype(BF16), cw_ref[...])
        ab_ref[0, rows] = ab[:, :GROUP_W].astype(BF16)
        ab_ref[1, rows] = ab[:, GROUP_W:].astype(BF16)

        ckvn = (_rms(ckv) * kvg_ref[...]).astype(BF16)
        kn = _dot(ckvn, wk_ref[...])
        kp = kp2[:, :LANES] * cosk_ref[rows] + kp2[:, LANES:] * sink_ref[rows]
        for hd in range(MLA_HEADS):
            lo = hd * HEAD_PAD
            k_ref[0, rows, lo:lo + HEAD_PAD] = (kn[:, lo:lo + HEAD_PAD] + kp).astype(BF16)
        vt_ref[0, :, rows] = lax.dot_general(wvt_ref[...], ckvn, _NT, preferred_element_type=F32).astype(BF16)


def _proj(x, scale, shift, g, win, cw, qg, wq, kvg, wk, wvt, cosq, sinq, cosk, sink, t):
    b, n, d = x.shape
    row = lambda bi, i: (bi, 0, 0)
    tile = lambda bi, i: (bi, i, 0)
    tab = pl.BlockSpec((t, LANES), lambda bi, i: (i, 0))
    qk_w = MLA_HEADS * HEAD_PAD
    return pl.pallas_call(
        _proj_kernel,
        out_shape=(jax.ShapeDtypeStruct((2, n, b * GROUP_W), BF16),
                   jax.ShapeDtypeStruct((b, n, 4 * GROUP_W), F32),
                   jax.ShapeDtypeStruct((b, n, qk_w), BF16),
                   jax.ShapeDtypeStruct((b, n, qk_w), BF16),
                   jax.ShapeDtypeStruct((b, GROUP_W, n), BF16)),
        grid=(b, n // t),
        in_specs=[pl.BlockSpec((1, t, d), tile),
                  pl.BlockSpec((1, 1, d), row),
                  pl.BlockSpec((1, 1, d), row),
                  _const_spec(g.shape), _const_spec(win.shape), _const_spec(cw.shape),
                  _const_spec(qg.shape), _const_spec(wq.shape), _const_spec(kvg.shape),
                  _const_spec(wk.shape), _const_spec(wvt.shape), tab, tab, tab, tab],
        out_specs=(pl.BlockSpec((2, t, GROUP_W), lambda bi, i: (0, i, bi)),
                   pl.BlockSpec((1, t, 4 * GROUP_W), tile),
                   pl.BlockSpec((1, t, qk_w), tile),
                   pl.BlockSpec((1, t, qk_w), tile),
                   pl.BlockSpec((1, GROUP_W, t), lambda bi, i: (bi, 0, i))),
        compiler_params=_params("parallel", "parallel"),
        name="proj",
    )(x, scale, shift, g, win, cw, qg, wq, kvg, wk, wvt, cosq, sinq, cosk, sink)


def _attn_kernel(n_sets, q_ref, *refs):
    o_ref = refs[2 * n_sets]
    s_ref = refs[2 * n_sets + 1]
    sizes = [refs[2 * j].shape[1] for j in range(n_sets)]
    offs = [sum(sizes[:j]) for j in range(n_sets)]

    def head_scores(hd):
        lo = hd * HEAD_PAD
        qh = q_ref[0, :, lo:lo + HEAD_PAD]
        for j in range(n_sets):
            s_ref[hd, offs[j]:offs[j] + sizes[j]] = lax.dot_general(
                refs[2 * j][0, :, lo:lo + HEAD_PAD], qh, _NT, preferred_element_type=F32)

    def head_out(hd):
        m = jnp.max(s_ref[hd], axis=0, keepdims=True)
        l = None
        o = None
        for j in range(n_sets):
            p = jnp.exp2(s_ref[hd, offs[j]:offs[j] + sizes[j]] - m)
            lj = jnp.sum(p, axis=0, keepdims=True)
            oj = _dot(refs[2 * j + 1][0, hd * MLA_V:(hd + 1) * MLA_V, :], p.astype(BF16))
            l = lj if l is None else l + lj
            o = oj if o is None else o + oj
        return o / l

    for hd in range(MLA_HEADS):
        head_scores(hd)
    outs = [head_out(hd) for hd in range(MLA_HEADS)]
    o_ref[0] = jnp.concatenate(outs, axis=0).T.astype(o_ref.dtype)


def _attn(q, kv_sets, tq):
    b, n, qk_w = q.shape
    in_specs = [pl.BlockSpec((1, tq, qk_w), lambda bi, i: (bi, i, 0))]
    args = [q]
    for k, vt in kv_sets:
        nk = k.shape[1]
        in_specs += [pl.BlockSpec((1, nk, qk_w), lambda bi, i: (bi, 0, 0)),
                     pl.BlockSpec((1, GROUP_W, nk), lambda bi, i: (bi, 0, 0))]
        args += [k, vt]
    return pl.pallas_call(
        functools.partial(_attn_kernel, len(kv_sets)),
        out_shape=jax.ShapeDtypeStruct((b, n, GROUP_W), BF16),
        grid=(b, n // tq),
        in_specs=in_specs,
        out_specs=pl.BlockSpec((1, tq, GROUP_W), lambda bi, i: (bi, i, 0)),
        scratch_shapes=[pltpu.VMEM((MLA_HEADS, sum(k.shape[1] for k, _ in kv_sets), tq), F32)],
        compiler_params=_params("parallel", "arbitrary"),
        name="attn",
    )(*args)


def _mixmlp_kernel(n_tokens, final, x_ref, cp_ref, prev_ref, next_ref, f_ref, at_ref, g1_ref, sc2_ref,
                   sh2_ref, g2_ref, convw_ref, poolw_ref, pscale_ref, wout_ref, n2g_ref, w1_ref, w2_ref,
                   fg_ref, o_ref):
    i = pl.program_id(1)
    t = x_ref.shape[1]
    first = i == 0
    last = i == pl.num_programs(1) - 1
    prev = jnp.where(first, 0.0, prev_ref[0])
    nxt = jnp.where(last, 0.0, next_ref[0])
    cpe = jnp.concatenate([prev, cp_ref[0], nxt], axis=0)
    ext = t + 2 * HALO
    inner = slice(HALO, HALO + t)
    lane = lax.broadcasted_iota(jnp.int32, (t, LANES), 1)
    tok = lax.broadcasted_iota(jnp.int32, (t, LANES), 0) + i * t
    lo_half = lane < SUB_W

    def shifted(a, k):
        return pltpu.roll(a, k % ext, axis=0)

    z = cpe[:, GROUP_W:2 * GROUP_W] * cpe[:, 2 * GROUP_W:3 * GROUP_W]
    y = shifted(z, 1) * convw_ref[0:1, :] + z * convw_ref[1:2, :] + shifted(z, -1) * convw_ref[2:3, :]
    conv = (cpe[:, :GROUP_W] * y)[inner]

    u = cpe[:, 3 * GROUP_W:]
    s2 = shifted(u, 1) + u
    s4 = shifted(s2, 1) + shifted(s2, -1)
    ub = u[:, LANES:]
    s4b = s4[:, LANES:]
    s8b = shifted(s4b, 2) + shifted(s4b, -2)
    s16b = shifted(s8b, 4) + shifted(s8b, -4)

    def mean(sum_lo, sum_hi, w_lo, w_hi):
        wh = jnp.where(lo_half, w_lo // 2, w_hi // 2)
        cnt = jnp.minimum(tok + wh - 1, n_tokens - 1) - jnp.maximum(tok - wh, 0) + 1
        return jnp.where(lo_half, sum_lo[inner], sum_hi[inner]) / cnt.astype(F32)

    pa = mean(s2[:, :LANES], s4[:, :LANES], POOL_WINDOWS[0], POOL_WINDOWS[1]) - u[inner, :LANES]
    pb = mean(s8b, s16b, POOL_WINDOWS[2], POOL_WINDOWS[3]) - ub[inner]
    pin = jnp.concatenate([pa, pb], axis=-1).astype(BF16)
    pool = _dot(pin, poolw_ref[...]) * pscale_ref[...]

    mix = jnp.concatenate([f_ref[...], conv.astype(BF16), pool.astype(BF16), at_ref[0]], axis=-1)
    x1 = x_ref[0] + g1_ref[0] * _dot(mix, wout_ref[...])

    h2 = (_rms(x1) * (n2g_ref[...] * (1.0 + sc2_ref[0])) + sh2_ref[0]).astype(BF16)
    chunk = D_FF // MLP_CHUNKS
    acc = None
    for c in range(MLP_CHUNKS):
        hid = jnp.maximum(_dot(h2, w1_ref[:, c * chunk:(c + 1) * chunk]), 0.0)
        part = _dot((hid * hid).astype(BF16), w2_ref[c * chunk:(c + 1) * chunk, :])
        acc = part if acc is None else acc + part
    x2 = x1 + g2_ref[0] * acc
    if final:
        x2 = _rms(x2) * fg_ref[...]
    o_ref[0] = x2


def _mixmlp(x, cp, f, at, g1, sc2, sh2, g2, convw, poolw, pscale, wout, n2g, w1, w2, fg, t, final):
    b, n, d = x.shape
    hb = t // HALO
    nhb = n // HALO
    row = lambda bi, i: (bi, 0, 0)
    tile = lambda bi, i: (bi, i, 0)
    row_spec = pl.BlockSpec((1, 1, d), row)
    return pl.pallas_call(
        functools.partial(_mixmlp_kernel, n, final),
        out_shape=jax.ShapeDtypeStruct((b, n, d), F32),
        grid=(b, n // t),
        in_specs=[pl.BlockSpec((1, t, d), tile),
                  pl.BlockSpec((1, t, 4 * GROUP_W), tile),
                  pl.BlockSpec((1, HALO, 4 * GROUP_W), lambda bi, i: (bi, jnp.maximum(i * hb - 1, 0), 0)),
                  pl.BlockSpec((1, HALO, 4 * GROUP_W), lambda bi, i: (bi, jnp.minimum((i + 1) * hb, nhb - 1), 0)),
                  pl.BlockSpec((t, GROUP_W), lambda bi, i: (i, bi)),
                  pl.BlockSpec((1, t, GROUP_W), tile),
                  row_spec, row_spec, row_spec, row_spec,
                  _const_spec(convw.shape), _const_spec(poolw.shape), _const_spec(pscale.shape),
                  _const_spec(wout.shape), _const_spec(n2g.shape), _const_spec(w1.shape),
                  _const_spec(w2.shape), _const_spec(fg.shape)],
        out_specs=pl.BlockSpec((1, t, d), tile),
        compiler_params=_params("parallel", "arbitrary"),
        name="mixmlp",
    )(x, cp, cp, cp, f, at, g1, sc2, sh2, g2, convw, poolw, pscale, wout, n2g, w1, w2, fg)


def _rope_rotate_cols(w):
    q = MLA_ROPE // 4
    return jnp.concatenate([-w[..., q:2 * q], w[..., :q], -w[..., 3 * q:], w[..., 2 * q:3 * q]], axis=-1)


def _layout_weights(w_in, w_uq, w_ukv, pool_w):
    d = w_in.shape[0]
    kr = w_in[:, OFF_MLA_KR:]
    z = lambda r, c: jnp.zeros((r, c), F32)
    pad_lo, pad_hi = MLA_NOPE, HEAD_PAD - MLA_NOPE - MLA_ROPE
    win = jnp.concatenate([w_in[:, :OFF_MLA_KR],
                           z(d, pad_lo), kr, z(d, pad_hi),
                           z(d, pad_lo), _rope_rotate_cols(kr), z(d, pad_hi)], axis=1)
    qh = w_uq.reshape(MLA_Q_RANK, MLA_HEADS, MLA_NOPE + MLA_ROPE)
    q_plain = jnp.pad(qh, ((0, 0), (0, 0), (0, pad_hi)))
    q_rot = jnp.pad(_rope_rotate_cols(qh[..., MLA_NOPE:]), ((0, 0), (0, 0), (pad_lo, pad_hi)))
    wq = jnp.concatenate([q_plain.reshape(MLA_Q_RANK, -1), q_rot.reshape(MLA_Q_RANK, -1)], axis=1)
    kvh = w_ukv.reshape(MLA_KV_RANK, MLA_HEADS, MLA_NOPE + MLA_V)
    k_plain = jnp.pad(kvh[..., :MLA_NOPE], ((0, 0), (0, 0), (0, HEAD_PAD - MLA_NOPE)))
    wk = k_plain.reshape(MLA_KV_RANK, -1)
    wvt = kvh[..., MLA_NOPE:].reshape(MLA_KV_RANK, -1).T
    poolw = jax.scipy.linalg.block_diag(*[pool_w[g] for g in range(N_SUB)])
    return win.astype(BF16), wq.astype(BF16), wk.astype(BF16), wvt.astype(BF16), poolw.astype(BF16)


def _rope_tables(n):
    rows = n // GRID_W
    row = jnp.repeat(jnp.arange(rows, dtype=F32), GRID_W)
    col = jnp.tile(jnp.arange(GRID_W, dtype=F32), rows)
    half = MLA_ROPE // 2
    inv = ROPE_BASE ** (-jnp.arange(0, half, 2, dtype=F32) / half)
    ang_r = row[:, None] * inv[None, :]
    ang_c = col[:, None] * inv[None, :]
    ang = jnp.concatenate([ang_r, ang_r, ang_c, ang_c], axis=-1)
    return _place_tables(jnp.cos(ang), jnp.sin(ang))


def _place_tables(cos, sin):
    n = cos.shape[0]
    qs = MLA_SCALE * LOG2E
    pad_hi = HEAD_PAD - MLA_NOPE - MLA_ROPE
    zl, zh = jnp.zeros((n, MLA_NOPE), F32), jnp.zeros((n, pad_hi), F32)
    cosq = jnp.concatenate([jnp.full((n, MLA_NOPE), qs, F32), cos * qs, zh], axis=1)
    sinq = jnp.concatenate([zl, sin * qs, zh], axis=1)
    cosk = jnp.concatenate([zl, cos, zh], axis=1)
    sink = jnp.concatenate([zl, sin, zh], axis=1)
    return cosq, sinq, cosk, sink


def _tile(n, want):
    return min(n, want)


def kernel(x, c, ctx, c_ctx, ada_w, ada_b, norm1_g, norm2_g, w_in, fourier_w, conv_w, pool_w, pool_scale,
           q_norm_g, w_uq, kv_norm_g, w_ukv, w_out, mlp_w1, mlp_w2, final_norm_g):
    b, n, d = x.shape
    nc = ctx.shape[1]
    depth = ada_w.shape[0]

    rows = -(-(b + 1) // SUBLANES) * SUBLANES
    cc = jnp.concatenate([c, c_ctx[None, :], jnp.zeros((rows - b - 1, d), F32)], axis=0)
    mod = _ada(cc, ada_w, ada_b)
    cw = _fold_fourier(fourier_w)
    g_x = _dft_matrix(n)
    g_c = _dft_matrix(nc)
    tabs_x = _rope_tables(n)
    tabs_c = _place_tables(jnp.ones((nc, MLA_ROPE), F32), jnp.zeros((nc, MLA_ROPE), F32))
    fg = final_norm_g.reshape(1, d)

    def stream_mod(l, ctx_stream):
        m = mod[l, b:b + 1] if ctx_stream else mod[l, :b]
        m = jnp.broadcast_to(m, (b, 6 * d)).reshape(b, 1, 6, d)
        return [m[:, :, j, :] for j in range(6)]

    for l in range(depth):
        last = l == depth - 1
        win, wq, wk, wvt, poolw = _layout_weights(w_in[l], w_uq[l], w_ukv[l], pool_w[l])
        n1g = norm1_g[l].reshape(1, d)
        n2g = norm2_g[l].reshape(1, d)
        qg = q_norm_g[l].reshape(1, -1)
        kvg = kv_norm_g[l].reshape(1, -1)
        pscale = pool_scale[l].reshape(1, -1)
        wout = w_out[l].astype(BF16)
        w1 = mlp_w1[l].astype(BF16)
        w2 = mlp_w2[l].astype(BF16)

        def project(tokens, ctx_stream, tabs):
            sh1, sc1 = stream_mod(l, ctx_stream)[:2]
            t = _tile(tokens.shape[1], 512)
            return _proj(tokens, sc1, sh1, n1g, win, cw[l], qg, wq, kvg, wk, wvt, *tabs, t)

        def finish(tokens, ctx_stream, cp, ab, g, q, kv_sets, final):
            nt = tokens.shape[1]
            _, _, g1, sh2, sc2, g2 = stream_mod(l, ctx_stream)
            f = _matmul(g, ab.reshape(2 * nt, b * GROUP_W), _tile(nt, 1024), _tile(b * GROUP_W, 2048),
                        _tile(2 * nt, 1024))
            at = _attn(q, kv_sets, _tile(nt, 256))
            return _mixmlp(tokens, cp, f, at, g1, sc2, sh2, g2, conv_w[l], poolw, pscale, wout, n2g, w1, w2,
                           fg, _tile(nt, 512), final)

        ab_c, cp_c, q_c, k_c, vt_c = project(ctx, True, tabs_c)
        ab_x, cp_x, q_x, k_x, vt_x = project(x, False, tabs_x)
        x = finish(x, False, cp_x, ab_x, g_x, q_x, [(k_x, vt_x), (k_c, vt_c)], last)
        if not last:
            ctx = finish(ctx, True, cp_c, ab_c, g_c, q_c, [(k_c, vt_c)], False)
    return x
```

```python
import functools
import math

import jax
import jax.numpy as jnp
from jax import lax
from jax.experimental import pallas as pl
from jax.experimental.pallas import tpu as pltpu

D_MODEL = 1024
DEPTH = 2
GRID_W = 64
GROUP_W = D_MODEL // 4
N_SUB = 4
SUB_W = GROUP_W // N_SUB
POOL_WINDOWS = (2, 4, 8, 16)
MLA_HEADS = 4
MLA_NOPE = GROUP_W // MLA_HEADS
MLA_ROPE = MLA_NOPE // 2
MLA_V = GROUP_W // MLA_HEADS
MLA_Q_RANK = GROUP_W
MLA_KV_RANK = GROUP_W // 2
MLA_SCALE = 1.0 / math.sqrt(MLA_NOPE + MLA_ROPE)
ROPE_BASE = 10000.0
D_FF = 4 * D_MODEL
EPS = 1e-6

OFF_CONV = GROUP_W
OFF_POOL = OFF_CONV + 3 * GROUP_W
OFF_MLA_Q = OFF_POOL + GROUP_W
OFF_MLA_KV = OFF_MLA_Q + MLA_Q_RANK
OFF_MLA_KR = OFF_MLA_KV + MLA_KV_RANK
IN_COLS = OFF_MLA_KR + MLA_ROPE

LANES = 128
SUBLANES = 8
HEAD_PAD = LANES
EXT_COLS = OFF_MLA_KR + 2 * LANES
DFT_ROWS = 64
HALO = SUBLANES
MLP_CHUNKS = 4
PROJ_SUB_ROWS = 256
VMEM_LIMIT = 52 * 1024 * 1024
ATTN_VMEM_LIMIT = 58 * 1024 * 1024
ATTN_TILE = 256
TOKEN_TILE = 512
LOG2E = math.log2(math.e)

F32 = jnp.float32
BF16 = jnp.bfloat16


_NT = (((1,), (1,)), ((), ()))


def _dot(a, b):
    return jnp.dot(a, b, preferred_element_type=F32)


def _rms(x):
    return x * lax.rsqrt(jnp.mean(x * x, axis=-1, keepdims=True) + EPS)


def _params(*sem):
    return pltpu.CompilerParams(dimension_semantics=sem, vmem_limit_bytes=VMEM_LIMIT)


def _const_spec(shape):
    nd = len(shape)
    return pl.BlockSpec(shape, lambda *_: (0,) * nd, pipeline_mode=pl.Buffered(1))


def _ada_kernel(cc_ref, w_ref, b_ref, o_ref):
    cc = cc_ref[...]
    s = cc * (1.0 / (1.0 + jnp.exp(-cc)))
    o_ref[0] = _dot(s.astype(BF16), w_ref[0].astype(BF16)) + b_ref[0]


def _ada(cc, ada_w, ada_b):
    depth, d, cols = ada_w.shape
    rows = cc.shape[0]
    tn = 1536
    return pl.pallas_call(
        _ada_kernel,
        out_shape=jax.ShapeDtypeStruct((depth, rows, cols), F32),
        grid=(depth, cols // tn),
        in_specs=[pl.BlockSpec((rows, d), lambda l, j: (0, 0)),
                  pl.BlockSpec((1, d, tn), lambda l, j: (l, 0, j)),
                  pl.BlockSpec((1, 1, tn), lambda l, j: (l, 0, j))],
        out_specs=pl.BlockSpec((1, rows, tn), lambda l, j: (l, 0, j)),
        compiler_params=_params("arbitrary", "arbitrary"),
        name="ada",
    )(cc, ada_w, ada_b.reshape(depth, 1, cols))


def _fold_kernel(dft_ref, w_ref, o_ref):
    w = w_ref[0]
    c = jnp.dot(dft_ref[:, :GROUP_W], w, preferred_element_type=F32, precision=lax.Precision.HIGHEST)
    s = jnp.dot(dft_ref[:, GROUP_W:], w, preferred_element_type=F32, precision=lax.Precision.HIGHEST)
    o_ref[0, :, :GROUP_W] = c.astype(BF16)
    o_ref[0, :, GROUP_W:] = s.astype(BF16)


def _channel_dft():
    i = jnp.arange(GROUP_W, dtype=jnp.int32)
    same = (i[:, None] // SUB_W) == (i[None, :] // SUB_W)
    ang = ((i[:, None] % SUB_W) * (i[None, :] % SUB_W) % SUB_W).astype(F32) * (2.0 * math.pi / SUB_W)
    scale = 1.0 / math.sqrt(SUB_W)
    c = jnp.where(same, jnp.cos(ang), 0.0) * scale
    s = jnp.where(same, jnp.sin(ang), 0.0) * scale
    return jnp.concatenate([c, s], axis=1)


def _fold_fourier(fourier_w):
    depth = fourier_w.shape[0]
    return pl.pallas_call(
        _fold_kernel,
        out_shape=jax.ShapeDtypeStruct((depth, GROUP_W, 2 * GROUP_W), BF16),
        grid=(depth,),
        in_specs=[pl.BlockSpec((GROUP_W, 2 * GROUP_W), lambda l: (0, 0)),
                  pl.BlockSpec((1, GROUP_W, GROUP_W), lambda l: (l, 0, 0))],
        out_specs=pl.BlockSpec((1, GROUP_W, 2 * GROUP_W), lambda l: (l, 0, 0)),
        compiler_params=_params("arbitrary"),
        name="fold_fourier",
    )(_channel_dft(), fourier_w)


def _dftgen_kernel(n, c1_ref, s1_ref, cb_ref, sb_ref, o_ref):
    c1 = c1_ref[0]
    s1 = s1_ref[0]
    cb = cb_ref[...]
    sb = sb_ref[...]
    scale = 1.0 / math.sqrt(n)
    o_ref[:, :n] = ((c1 * cb - s1 * sb) * scale).astype(BF16)
    o_ref[:, n:] = ((s1 * cb + c1 * sb) * (-scale)).astype(BF16)


def _dft_matrix(n):
    nb = n // DFT_ROWS
    k = jnp.arange(n, dtype=jnp.int32)
    j1 = jnp.arange(nb, dtype=jnp.int32) * DFT_ROWS
    j0 = jnp.arange(DFT_ROWS, dtype=jnp.int32)
    w = 2.0 * math.pi / n
    a1 = ((j1[:, None] * k[None, :]) % n).astype(F32) * w
    a0 = ((j0[:, None] * k[None, :]) % n).astype(F32) * w
    row_spec = pl.BlockSpec((1, 1, n), lambda i: (i, 0, 0))
    tab_spec = pl.BlockSpec((DFT_ROWS, n), lambda i: (0, 0))
    return pl.pallas_call(
        functools.partial(_dftgen_kernel, n),
        out_shape=jax.ShapeDtypeStruct((n, 2 * n), BF16),
        grid=(nb,),
        in_specs=[row_spec, row_spec, tab_spec, tab_spec],
        out_specs=pl.BlockSpec((DFT_ROWS, 2 * n), lambda i: (i, 0)),
        compiler_params=_params("arbitrary"),
        name="dft_matrix",
    )(jnp.cos(a1).reshape(nb, 1, n), jnp.sin(a1).reshape(nb, 1, n), jnp.cos(a0), jnp.sin(a0))


def _matmul_kernel(a_ref, b_ref, o_ref, acc_ref):
    k = pl.program_id(2)

    @pl.when(k == 0)
    def _():
        acc_ref[...] = jnp.zeros_like(acc_ref)

    acc_ref[...] += _dot(a_ref[...], b_ref[...])

    @pl.when(k == pl.num_programs(2) - 1)
    def _():
        o_ref[...] = acc_ref[...].astype(o_ref.dtype)


def _matmul(a, b, tm, tn, tk):
    m, kk = a.shape
    n = b.shape[1]
    return pl.pallas_call(
        _matmul_kernel,
        out_shape=jax.ShapeDtypeStruct((m, n), BF16),
        grid=(m // tm, n // tn, kk // tk),
        in_specs=[pl.BlockSpec((tm, tk), lambda i, j, k: (i, k)),
                  pl.BlockSpec((tk, tn), lambda i, j, k: (k, j))],
        out_specs=pl.BlockSpec((tm, tn), lambda i, j, k: (i, j)),
        scratch_shapes=[pltpu.VMEM((tm, tn), F32)],
        compiler_params=_params("parallel", "parallel", "arbitrary"),
        name="dft_matmul",
    )(a, b)


def _proj_kernel(x_ref, sc_ref, sh_ref, g_ref, win_ref, cw_ref, qg_ref, wq_ref, kvg_ref, wk_ref, wvt_ref,
                 cosq_ref, sinq_ref, cosk_ref, sink_ref, ab_ref, cp_ref, q_ref, k_ref, vt_ref):
    gain = g_ref[...] * (1.0 + sc_ref[0])
    shift = sh_ref[0]
    half = MLA_HEADS * HEAD_PAD
    t = x_ref.shape[1]
    sub = min(t, PROJ_SUB_ROWS)
    for r0 in range(0, t, sub):
        rows = slice(r0, r0 + sub)
        hb = (_rms(x_ref[0, rows]) * gain + shift).astype(BF16)

        cq = _dot(hb, win_ref[:, OFF_MLA_Q:OFF_MLA_KV])
        ckv = _dot(hb, win_ref[:, OFF_MLA_KV:OFF_MLA_KR])
        u = _dot(hb, win_ref[:, :OFF_CONV])
        kp2 = _dot(hb, win_ref[:, OFF_MLA_KR:EXT_COLS])
        cp_ref[0, rows] = _dot(hb, win_ref[:, OFF_CONV:OFF_MLA_Q])

        cqn = (_rms(cq) * qg_ref[...]).astype(BF16)
        qq = _dot(cqn, wq_ref[...])
        cosq = cosq_ref[rows]
        sinq = sinq_ref[rows]
        for hd in range(MLA_HEADS):
            lo = hd * HEAD_PAD
            q_ref[0, rows, lo:lo + HEAD_PAD] = (qq[:, lo:lo + HEAD_PAD] * cosq
                                                 + qq[:, half + lo:half + lo + HEAD_PAD] * sinq).astype(BF16)

        ab = _dot(u.astype(BF16), cw_ref[...])
        ab_ref[0, rows] = ab[:, :GROUP_W].astype(BF16)
        ab_ref[1, rows] = ab[:, GROUP_W:].astype(BF16)

        ckvn = (_rms(ckv) * kvg_ref[...]).astype(BF16)
        kn = _dot(ckvn, wk_ref[...])
        kp = kp2[:, :LANES] * cosk_ref[rows] + kp2[:, LANES:] * sink_ref[rows]
        for hd in range(MLA_HEADS):
            lo = hd * HEAD_PAD
            k_ref[0, rows, lo:lo + HEAD_PAD] = (kn[:, lo:lo + HEAD_PAD] + kp).astype(BF16)
        vt_ref[0, :, rows] = lax.dot_general(wvt_ref[...], ckvn, _NT, preferred_element_type=F32).astype(BF16)


def _proj(x, scale, shift, g, win, cw, qg, wq, kvg, wk, wvt, cosq, sinq, cosk, sink, t):
    b, n, d = x.shape
    row = lambda bi, i: (bi, 0, 0)
    tile = lambda bi, i: (bi, i, 0)
    tab = pl.BlockSpec((t, LANES), lambda bi, i: (i, 0))
    qk_w = MLA_HEADS * HEAD_PAD
    return pl.pallas_call(
        _proj_kernel,
        out_shape=(jax.ShapeDtypeStruct((2, n, b * GROUP_W), BF16),
                   jax.ShapeDtypeStruct((b, n, 4 * GROUP_W), F32),
                   jax.ShapeDtypeStruct((b, n, qk_w), BF16),
                   jax.ShapeDtypeStruct((b, n, qk_w), BF16),
                   jax.ShapeDtypeStruct((b, GROUP_W, n), BF16)),
        grid=(b, n // t),
        in_specs=[pl.BlockSpec((1, t, d), tile),
                  pl.BlockSpec((1, 1, d), row),
                  pl.BlockSpec((1, 1, d), row),
                  _const_spec(g.shape), _const_spec(win.shape), _const_spec(cw.shape),
                  _const_spec(qg.shape), _const_spec(wq.shape), _const_spec(kvg.shape),
                  _const_spec(wk.shape), _const_spec(wvt.shape), tab, tab, tab, tab],
        out_specs=(pl.BlockSpec((2, t, GROUP_W), lambda bi, i: (0, i, bi)),
                   pl.BlockSpec((1, t, 4 * GROUP_W), tile),
                   pl.BlockSpec((1, t, qk_w), tile),
                   pl.BlockSpec((1, t, qk_w), tile),
                   pl.BlockSpec((1, GROUP_W, t), lambda bi, i: (bi, 0, i))),
        compiler_params=_params("parallel", "parallel"),
        name="proj",
    )(x, scale, shift, g, win, cw, qg, wq, kvg, wk, wvt, cosq, sinq, cosk, sink)


def _unit_scores(q_ref, k_refs, heads, s_ref, m_ref):
    for i, hd in enumerate(heads):
        lo = hd * HEAD_PAD
        qh = q_ref[0, :, lo:lo + HEAD_PAD]
        off = 0
        m = None
        for k_ref in k_refs:
            nk = k_ref.shape[1]
            s = lax.dot_general(k_ref[0, :, lo:lo + HEAD_PAD], qh, _NT, preferred_element_type=F32)
            s_ref[i, off:off + nk] = s
            mj = jnp.max(s, axis=0, keepdims=True)
            m = mj if m is None else jnp.maximum(m, mj)
            off += nk
        m_ref[i] = m


def _unit_softmax_pv(s_ref, m_ref, vt_refs, heads, o_ref):
    outs = []
    for i, hd in enumerate(heads):
        m = m_ref[i]
        l = None
        o = None
        off = 0
        for vt_ref in vt_refs:
            nk = vt_ref.shape[2]
            p = jnp.exp2(s_ref[i, off:off + nk] - m)
            lj = jnp.sum(p, axis=0, keepdims=True)
            oj = _dot(vt_ref[0, hd * MLA_V:(hd + 1) * MLA_V, :], p.astype(BF16))
            l = lj if l is None else l + lj
            o = oj if o is None else o + oj
            off += nk
        outs.append(o / l)
    o_ref[0] = jnp.concatenate(outs, axis=0).T.astype(o_ref.dtype)


def _unit_scratch(n_heads, n_keys, tq):
    return [pltpu.VMEM((n_heads, n_keys, tq), F32), pltpu.VMEM((n_heads, 1, tq), F32)]


def _attn_kernel(n_sets, q_ref, *refs):
    o_ref, s_ref, m_ref = refs[2 * n_sets:]
    heads = tuple(range(MLA_HEADS))
    _unit_scores(q_ref, refs[:n_sets], heads, s_ref, m_ref)
    _unit_softmax_pv(s_ref, m_ref, refs[n_sets:2 * n_sets], heads, o_ref)


def _attn(q, kv_sets, tq):
    b, n, qk_w = q.shape
    whole = lambda bi, i: (bi, 0, 0)
    ks = [k for k, _ in kv_sets]
    vts = [vt for _, vt in kv_sets]
    return pl.pallas_call(
        functools.partial(_attn_kernel, len(kv_sets)),
        out_shape=jax.ShapeDtypeStruct((b, n, GROUP_W), BF16),
        grid=(b, n // tq),
        in_specs=([pl.BlockSpec((1, tq, qk_w), lambda bi, i: (bi, i, 0))]
                  + [pl.BlockSpec((1,) + k.shape[1:], whole) for k in ks]
                  + [pl.BlockSpec((1,) + vt.shape[1:], whole) for vt in vts]),
        out_specs=pl.BlockSpec((1, tq, GROUP_W), lambda bi, i: (bi, i, 0)),
        scratch_shapes=_unit_scratch(MLA_HEADS, sum(k.shape[1] for k in ks), tq),
        compiler_params=_params("parallel", "arbitrary"),
        name="attn",
    )(q, *ks, *vts)


def _attn_pipe_kernel(n_sets, qe_ref, qo_ref, *refs):
    k_refs = refs[:n_sets]
    vt_prev = refs[n_sets:2 * n_sets]
    vt_cur = refs[2 * n_sets:3 * n_sets]
    oe_ref, oo_ref, s0, m0, s1, m1 = refs[3 * n_sets:]
    heads = tuple(range(MLA_HEADS))

    @pl.when(pl.program_id(0) == 0)
    def _():
        s1[...] = jnp.zeros_like(s1)
        m1[...] = jnp.zeros_like(m1)

    _unit_scores(qe_ref, k_refs, heads, s0, m0)
    _unit_softmax_pv(s1, m1, vt_prev, heads, oo_ref)
    _unit_scores(qo_ref, k_refs, heads, s1, m1)
    _unit_softmax_pv(s0, m0, vt_cur, heads, oe_ref)


def _attn_pipelined(q, kv_sets, tq):
    b, n, qk_w = q.shape
    nq = n // tq
    pairs = b * nq // 2
    ks = [k for k, _ in kv_sets]
    vts = [vt for _, vt in kv_sets]
    even = lambda j: jnp.minimum(2 * j, 2 * pairs - 2)
    b_cur = lambda j: even(j) // nq
    b_prev = lambda j: jnp.maximum(2 * j - 1, 0) // nq
    cur = lambda j: (jnp.minimum(j, pairs - 1), 0, 0)
    prev = lambda j: (jnp.maximum(j - 1, 0), 0, 0)
    once = pl.Buffered(1)
    out = jax.ShapeDtypeStruct((pairs, tq, GROUP_W), BF16)
    out_block = (1, tq, GROUP_W)
    n_keys = sum(k.shape[1] for k in ks)
    even_tiles, odd_tiles = pl.pallas_call(
        functools.partial(_attn_pipe_kernel, len(kv_sets)),
        out_shape=(out, out),
        grid=(pairs + 1,),
        in_specs=([pl.BlockSpec((1, tq, qk_w), lambda j: (b_cur(j), even(j) % nq, 0)),
                   pl.BlockSpec((1, tq, qk_w), lambda j: (b_cur(j), even(j) % nq + 1, 0))]
                  + [pl.BlockSpec((1,) + k.shape[1:], lambda j: (b_cur(j), 0, 0), pipeline_mode=once) for k in ks]
                  + [pl.BlockSpec((1,) + vt.shape[1:], lambda j: (b_prev(j), 0, 0), pipeline_mode=once)
                     for vt in vts]
                  + [pl.BlockSpec((1,) + vt.shape[1:], lambda j: (b_cur(j), 0, 0), pipeline_mode=once)
                     for vt in vts]),
        out_specs=(pl.BlockSpec(out_block, cur), pl.BlockSpec(out_block, prev)),
        scratch_shapes=_unit_scratch(MLA_HEADS, n_keys, tq) * 2,
        compiler_params=pltpu.CompilerParams(dimension_semantics=("arbitrary",),
                                             vmem_limit_bytes=ATTN_VMEM_LIMIT),
        name="attn_pipe",
    )(q, q, *ks, *vts, *vts)
    return [[even_tiles], [odd_tiles]]


def _mixmlp_kernel(n_tokens, final, x_ref, cp_ref, prev_ref, next_ref, f_ref, g1_ref, sc2_ref,
                   sh2_ref, g2_ref, convw_ref, poolw_ref, pscale_ref, wout_ref, n2g_ref, w1_ref, w2_ref,
                   fg_ref, *at_and_out):
    at_refs, o_ref = at_and_out[:-1], at_and_out[-1]
    i = pl.program_id(1)
    t = x_ref.shape[1]
    first = i == 0
    last = i == pl.num_programs(1) - 1
    prev = jnp.where(first, 0.0, prev_ref[0])
    nxt = jnp.where(last, 0.0, next_ref[0])
    cpe = jnp.concatenate([prev, cp_ref[0], nxt], axis=0)
    ext = t + 2 * HALO
    inner = slice(HALO, HALO + t)
    lane = lax.broadcasted_iota(jnp.int32, (t, LANES), 1)
    tok = lax.broadcasted_iota(jnp.int32, (t, LANES), 0) + i * t
    lo_half = lane < SUB_W

    def shifted(a, k):
        return pltpu.roll(a, k % ext, axis=0)

    z = cpe[:, GROUP_W:2 * GROUP_W] * cpe[:, 2 * GROUP_W:3 * GROUP_W]
    y = shifted(z, 1) * convw_ref[0:1, :] + z * convw_ref[1:2, :] + shifted(z, -1) * convw_ref[2:3, :]
    conv = (cpe[:, :GROUP_W] * y)[inner]

    u = cpe[:, 3 * GROUP_W:]
    s2 = shifted(u, 1) + u
    s4 = shifted(s2, 1) + shifted(s2, -1)
    ub = u[:, LANES:]
    s4b = s4[:, LANES:]
    s8b = shifted(s4b, 2) + shifted(s4b, -2)
    s16b = shifted(s8b, 4) + shifted(s8b, -4)

    def mean(sum_lo, sum_hi, w_lo, w_hi):
        wh = jnp.where(lo_half, w_lo // 2, w_hi // 2)
        cnt = jnp.minimum(tok + wh - 1, n_tokens - 1) - jnp.maximum(tok - wh, 0) + 1
        return jnp.where(lo_half, sum_lo[inner], sum_hi[inner]) / cnt.astype(F32)

    pa = mean(s2[:, :LANES], s4[:, :LANES], POOL_WINDOWS[0], POOL_WINDOWS[1]) - u[inner, :LANES]
    pb = mean(s8b, s16b, POOL_WINDOWS[2], POOL_WINDOWS[3]) - ub[inner]
    pin = jnp.concatenate([pa, pb], axis=-1).astype(BF16)
    pool = _dot(pin, poolw_ref[...]) * pscale_ref[...]

    per_row = GROUP_W // at_refs[0].shape[2]
    attn = jnp.concatenate([jnp.concatenate([r[0] for r in at_refs[k:k + per_row]], axis=-1)
                            for k in range(0, len(at_refs), per_row)], axis=0)
    mix = jnp.concatenate([f_ref[...], conv.astype(BF16), pool.astype(BF16), attn], axis=-1)
    x1 = x_ref[0] + g1_ref[0] * _dot(mix, wout_ref[...])

    h2 = (_rms(x1) * (n2g_ref[...] * (1.0 + sc2_ref[0])) + sh2_ref[0]).astype(BF16)
    chunk = D_FF // MLP_CHUNKS
    acc = None
    for c in range(MLP_CHUNKS):
        hid = jnp.maximum(_dot(h2, w1_ref[:, c * chunk:(c + 1) * chunk]), 0.0)
        part = _dot((hid * hid).astype(BF16), w2_ref[c * chunk:(c + 1) * chunk, :])
        acc = part if acc is None else acc + part
    x2 = x1 + g2_ref[0] * acc
    if final:
        x2 = _rms(x2) * fg_ref[...]
    o_ref[0] = x2


def _mixmlp(x, cp, f, ats, g1, sc2, sh2, g2, convw, poolw, pscale, wout, n2g, w1, w2, fg, t, final):
    b, n, d = x.shape
    at_specs = [pl.BlockSpec((1,) + a.shape[1:], lambda bi, i: (bi * (n // t) + i, 0, 0)) for r in ats for a in r]
    ats = [a for r in ats for a in r]
    hb = t // HALO
    nhb = n // HALO
    row = lambda bi, i: (bi, 0, 0)
    tile = lambda bi, i: (bi, i, 0)
    row_spec = pl.BlockSpec((1, 1, d), row)
    return pl.pallas_call(
        functools.partial(_mixmlp_kernel, n, final),
        out_shape=jax.ShapeDtypeStruct((b, n, d), F32),
        grid=(b, n // t),
        in_specs=[pl.BlockSpec((1, t, d), tile),
                  pl.BlockSpec((1, t, 4 * GROUP_W), tile),
                  pl.BlockSpec((1, HALO, 4 * GROUP_W), lambda bi, i: (bi, jnp.maximum(i * hb - 1, 0), 0)),
                  pl.BlockSpec((1, HALO, 4 * GROUP_W), lambda bi, i: (bi, jnp.minimum((i + 1) * hb, nhb - 1), 0)),
                  pl.BlockSpec((t, GROUP_W), lambda bi, i: (i, bi)),
                  row_spec, row_spec, row_spec, row_spec,
                  _const_spec(convw.shape), _const_spec(poolw.shape), _const_spec(pscale.shape),
                  _const_spec(wout.shape), _const_spec(n2g.shape), _const_spec(w1.shape),
                  _const_spec(w2.shape), _const_spec(fg.shape)] + at_specs,
        out_specs=pl.BlockSpec((1, t, d), tile),
        compiler_params=_params("parallel", "arbitrary"),
        name="mixmlp",
    )(x, cp, cp, cp, f, g1, sc2, sh2, g2, convw, poolw, pscale, wout, n2g, w1, w2, fg, *ats)


def _rope_rotate_cols(w):
    q = MLA_ROPE // 4
    return jnp.concatenate([-w[..., q:2 * q], w[..., :q], -w[..., 3 * q:], w[..., 2 * q:3 * q]], axis=-1)


def _layout_weights(w_in, w_uq, w_ukv, pool_w):
    d = w_in.shape[0]
    kr = w_in[:, OFF_MLA_KR:]
    z = lambda r, c: jnp.zeros((r, c), F32)
    pad_lo, pad_hi = MLA_NOPE, HEAD_PAD - MLA_NOPE - MLA_ROPE
    win = jnp.concatenate([w_in[:, :OFF_MLA_KR],
                           z(d, pad_lo), kr, z(d, pad_hi),
                           z(d, pad_lo), _rope_rotate_cols(kr), z(d, pad_hi)], axis=1)
    qh = w_uq.reshape(MLA_Q_RANK, MLA_HEADS, MLA_NOPE + MLA_ROPE)
    q_plain = jnp.pad(qh, ((0, 0), (0, 0), (0, pad_hi)))
    q_rot = jnp.pad(_rope_rotate_cols(qh[..., MLA_NOPE:]), ((0, 0), (0, 0), (pad_lo, pad_hi)))
    wq = jnp.concatenate([q_plain.reshape(MLA_Q_RANK, -1), q_rot.reshape(MLA_Q_RANK, -1)], axis=1)
    kvh = w_ukv.reshape(MLA_KV_RANK, MLA_HEADS, MLA_NOPE + MLA_V)
    k_plain = jnp.pad(kvh[..., :MLA_NOPE], ((0, 0), (0, 0), (0, HEAD_PAD - MLA_NOPE)))
    wk = k_plain.reshape(MLA_KV_RANK, -1)
    wvt = kvh[..., MLA_NOPE:].reshape(MLA_KV_RANK, -1).T
    poolw = jax.scipy.linalg.block_diag(*[pool_w[g] for g in range(N_SUB)])
    return win.astype(BF16), wq.astype(BF16), wk.astype(BF16), wvt.astype(BF16), poolw.astype(BF16)


def _rope_tables(n):
    rows = n // GRID_W
    row = jnp.repeat(jnp.arange(rows, dtype=F32), GRID_W)
    col = jnp.tile(jnp.arange(GRID_W, dtype=F32), rows)
    half = MLA_ROPE // 2
    inv = ROPE_BASE ** (-jnp.arange(0, half, 2, dtype=F32) / half)
    ang_r = row[:, None] * inv[None, :]
    ang_c = col[:, None] * inv[None, :]
    ang = jnp.concatenate([ang_r, ang_r, ang_c, ang_c], axis=-1)
    return _place_tables(jnp.cos(ang), jnp.sin(ang))


def _place_tables(cos, sin):
    n = cos.shape[0]
    qs = MLA_SCALE * LOG2E
    pad_hi = HEAD_PAD - MLA_NOPE - MLA_ROPE
    zl, zh = jnp.zeros((n, MLA_NOPE), F32), jnp.zeros((n, pad_hi), F32)
    cosq = jnp.concatenate([jnp.full((n, MLA_NOPE), qs, F32), cos * qs, zh], axis=1)
    sinq = jnp.concatenate([zl, sin * qs, zh], axis=1)
    cosk = jnp.concatenate([zl, cos, zh], axis=1)
    sink = jnp.concatenate([zl, sin, zh], axis=1)
    return cosq, sinq, cosk, sink


def _tile(n, want):
    return min(n, want)


def kernel(x, c, ctx, c_ctx, ada_w, ada_b, norm1_g, norm2_g, w_in, fourier_w, conv_w, pool_w, pool_scale,
           q_norm_g, w_uq, kv_norm_g, w_ukv, w_out, mlp_w1, mlp_w2, final_norm_g):
    b, n, d = x.shape
    nc = ctx.shape[1]
    depth = ada_w.shape[0]

    rows = -(-(b + 1) // SUBLANES) * SUBLANES
    cc = jnp.concatenate([c, c_ctx[None, :], jnp.zeros((rows - b - 1, d), F32)], axis=0)
    mod = _ada(cc, ada_w, ada_b)
    cw = _fold_fourier(fourier_w)
    g_x = _dft_matrix(n)
    g_c = _dft_matrix(nc)
    tabs_x = _rope_tables(n)
    tabs_c = _place_tables(jnp.ones((nc, MLA_ROPE), F32), jnp.zeros((nc, MLA_ROPE), F32))
    fg = final_norm_g.reshape(1, d)

    def stream_mod(l, ctx_stream):
        m = mod[l, b:b + 1] if ctx_stream else mod[l, :b]
        m = jnp.broadcast_to(m, (b, 6 * d)).reshape(b, 1, 6, d)
        return [m[:, :, j, :] for j in range(6)]

    for l in range(depth):
        last = l == depth - 1
        win, wq, wk, wvt, poolw = _layout_weights(w_in[l], w_uq[l], w_ukv[l], pool_w[l])
        n1g = norm1_g[l].reshape(1, d)
        n2g = norm2_g[l].reshape(1, d)
        qg = q_norm_g[l].reshape(1, -1)
        kvg = kv_norm_g[l].reshape(1, -1)
        pscale = pool_scale[l].reshape(1, -1)
        wout = w_out[l].astype(BF16)
        w1 = mlp_w1[l].astype(BF16)
        w2 = mlp_w2[l].astype(BF16)

        def project(tokens, ctx_stream, tabs):
            sh1, sc1 = stream_mod(l, ctx_stream)[:2]
            t = _tile(tokens.shape[1], TOKEN_TILE)
            return _proj(tokens, sc1, sh1, n1g, win, cw[l], qg, wq, kvg, wk, wvt, *tabs, t)

        def finish(tokens, ctx_stream, cp, ab, g, q, kv_sets, final):
            nt = tokens.shape[1]
            t = _tile(nt, TOKEN_TILE)
            _, _, g1, sh2, sc2, g2 = stream_mod(l, ctx_stream)
            f = _matmul(g, ab.reshape(2 * nt, b * GROUP_W), _tile(nt, 1024), _tile(b * GROUP_W, 2048),
                        _tile(2 * nt, 1024))
            if t == 2 * ATTN_TILE:
                ats = _attn_pipelined(q, kv_sets, ATTN_TILE)
            else:
                ats = [[_attn(q, kv_sets, t).reshape(b * nt // t, t, GROUP_W)]]
            return _mixmlp(tokens, cp, f, ats, g1, sc2, sh2, g2, conv_w[l], poolw, pscale, wout, n2g, w1, w2,
                           fg, t, final)

        ab_c, cp_c, q_c, k_c, vt_c = project(ctx, True, tabs_c)
        ab_x, cp_x, q_x, k_x, vt_x = project(x, False, tabs_x)
        x = finish(x, False, cp_x, ab_x, g_x, q_x, [(k_x, vt_x), (k_c, vt_c)], last)
        if not last:
            ctx = finish(ctx, True, cp_c, ab_c, g_c, q_c, [(k_c, vt_c)], False)
    return x
```

```python
import functools
import math

import jax
import jax.numpy as jnp
from jax import lax
from jax.experimental import pallas as pl
from jax.experimental.pallas import tpu as pltpu

D_MODEL = 1024
DEPTH = 2
GRID_W = 64
GROUP_W = D_MODEL // 4
N_SUB = 4
SUB_W = GROUP_W // N_SUB
POOL_WINDOWS = (2, 4, 8, 16)
MLA_HEADS = 4
MLA_NOPE = GROUP_W // MLA_HEADS
MLA_ROPE = MLA_NOPE // 2
MLA_V = GROUP_W // MLA_HEADS
MLA_Q_RANK = GROUP_W
MLA_KV_RANK = GROUP_W // 2
MLA_SCALE = 1.0 / math.sqrt(MLA_NOPE + MLA_ROPE)
ROPE_BASE = 10000.0
D_FF = 4 * D_MODEL
EPS = 1e-6

OFF_CONV = GROUP_W
OFF_POOL = OFF_CONV + 3 * GROUP_W
OFF_MLA_Q = OFF_POOL + GROUP_W
OFF_MLA_KV = OFF_MLA_Q + MLA_Q_RANK
OFF_MLA_KR = OFF_MLA_KV + MLA_KV_RANK
IN_COLS = OFF_MLA_KR + MLA_ROPE

LANES = 128
SUBLANES = 8
HEAD_PAD = LANES
EXT_COLS = OFF_MLA_KR + 2 * LANES
DFT_ROWS = 64
HALO = SUBLANES
MLP_CHUNKS = 4
PROJ_SUB_ROWS = 256
VMEM_LIMIT = 52 * 1024 * 1024
ATTN_VMEM_LIMIT = 58 * 1024 * 1024
ATTN_TILE = 256
TOKEN_TILE = 512
LOG2E = math.log2(math.e)

F32 = jnp.float32
BF16 = jnp.bfloat16


_NT = (((1,), (1,)), ((), ()))


def _dot(a, b):
    return jnp.dot(a, b, preferred_element_type=F32)


def _rms(x):
    return x * lax.rsqrt(jnp.mean(x * x, axis=-1, keepdims=True) + EPS)


def _params(*sem):
    return pltpu.CompilerParams(dimension_semantics=sem, vmem_limit_bytes=VMEM_LIMIT)


def _const_spec(shape):
    nd = len(shape)
    return pl.BlockSpec(shape, lambda *_: (0,) * nd, pipeline_mode=pl.Buffered(1))


def _ada_kernel(cc_ref, w_ref, b_ref, o_ref):
    cc = cc_ref[...]
    s = cc * (1.0 / (1.0 + jnp.exp(-cc)))
    o_ref[0] = _dot(s.astype(BF16), w_ref[0].astype(BF16)) + b_ref[0]


def _ada(cc, ada_w, ada_b):
    depth, d, cols = ada_w.shape
    rows = cc.shape[0]
    tn = 1536
    return pl.pallas_call(
        _ada_kernel,
        out_shape=jax.ShapeDtypeStruct((depth, rows, cols), F32),
        grid=(depth, cols // tn),
        in_specs=[pl.BlockSpec((rows, d), lambda l, j: (0, 0)),
                  pl.BlockSpec((1, d, tn), lambda l, j: (l, 0, j)),
                  pl.BlockSpec((1, 1, tn), lambda l, j: (l, 0, j))],
        out_specs=pl.BlockSpec((1, rows, tn), lambda l, j: (l, 0, j)),
        compiler_params=_params("arbitrary", "arbitrary"),
        name="ada",
    )(cc, ada_w, ada_b.reshape(depth, 1, cols))


def _fold_kernel(dft_ref, w_ref, o_ref):
    w = w_ref[0]
    c = jnp.dot(dft_ref[:, :GROUP_W], w, preferred_element_type=F32, precision=lax.Precision.HIGHEST)
    s = jnp.dot(dft_ref[:, GROUP_W:], w, preferred_element_type=F32, precision=lax.Precision.HIGHEST)
    o_ref[0, :, :GROUP_W] = c.astype(BF16)
    o_ref[0, :, GROUP_W:] = s.astype(BF16)


def _channel_dft():
    i = jnp.arange(GROUP_W, dtype=jnp.int32)
    same = (i[:, None] // SUB_W) == (i[None, :] // SUB_W)
    ang = ((i[:, None] % SUB_W) * (i[None, :] % SUB_W) % SUB_W).astype(F32) * (2.0 * math.pi / SUB_W)
    scale = 1.0 / math.sqrt(SUB_W)
    c = jnp.where(same, jnp.cos(ang), 0.0) * scale
    s = jnp.where(same, jnp.sin(ang), 0.0) * scale
    return jnp.concatenate([c, s], axis=1)


def _fold_fourier(fourier_w):
    depth = fourier_w.shape[0]
    return pl.pallas_call(
        _fold_kernel,
        out_shape=jax.ShapeDtypeStruct((depth, GROUP_W, 2 * GROUP_W), BF16),
        grid=(depth,),
        in_specs=[pl.BlockSpec((GROUP_W, 2 * GROUP_W), lambda l: (0, 0)),
                  pl.BlockSpec((1, GROUP_W, GROUP_W), lambda l: (l, 0, 0))],
        out_specs=pl.BlockSpec((1, GROUP_W, 2 * GROUP_W), lambda l: (l, 0, 0)),
        compiler_params=_params("arbitrary"),
        name="fold_fourier",
    )(_channel_dft(), fourier_w)


def _dftgen_kernel(n, c1_ref, s1_ref, cb_ref, sb_ref, o_ref):
    c1 = c1_ref[0]
    s1 = s1_ref[0]
    cb = cb_ref[...]
    sb = sb_ref[...]
    scale = 1.0 / math.sqrt(n)
    o_ref[:, :n] = ((c1 * cb - s1 * sb) * scale).astype(BF16)
    o_ref[:, n:] = ((s1 * cb + c1 * sb) * (-scale)).astype(BF16)


def _dft_matrix(n, rows):
    nb = rows // DFT_ROWS
    k = jnp.arange(n, dtype=jnp.int32)
    j1 = jnp.arange(nb, dtype=jnp.int32) * DFT_ROWS
    j0 = jnp.arange(DFT_ROWS, dtype=jnp.int32)
    w = 2.0 * math.pi / n
    a1 = ((j1[:, None] * k[None, :]) % n).astype(F32) * w
    a0 = ((j0[:, None] * k[None, :]) % n).astype(F32) * w
    row_spec = pl.BlockSpec((1, 1, n), lambda i: (i, 0, 0))
    tab_spec = pl.BlockSpec((DFT_ROWS, n), lambda i: (0, 0))
    return pl.pallas_call(
        functools.partial(_dftgen_kernel, n),
        out_shape=jax.ShapeDtypeStruct((rows, 2 * n), BF16),
        grid=(nb,),
        in_specs=[row_spec, row_spec, tab_spec, tab_spec],
        out_specs=pl.BlockSpec((DFT_ROWS, 2 * n), lambda i: (i, 0)),
        compiler_params=_params("arbitrary"),
        name="dft_matrix",
    )(jnp.cos(a1).reshape(nb, 1, n), jnp.sin(a1).reshape(nb, 1, n), jnp.cos(a0), jnp.sin(a0))


def _matmul_kernel(a_ref, b_ref, o_ref, acc_ref):
    k = pl.program_id(2)

    @pl.when(k == 0)
    def _():
        acc_ref[...] = jnp.zeros_like(acc_ref)

    acc_ref[...] += _dot(a_ref[...], b_ref[...])

    @pl.when(k == pl.num_programs(2) - 1)
    def _():
        o_ref[...] = acc_ref[...].astype(o_ref.dtype)


def _matmul(a, b, tm, tn, tk):
    m, kk = a.shape
    n = b.shape[1]
    return pl.pallas_call(
        _matmul_kernel,
        out_shape=jax.ShapeDtypeStruct((m, n), BF16),
        grid=(m // tm, n // tn, kk // tk),
        in_specs=[pl.BlockSpec((tm, tk), lambda i, j, k: (i, k)),
                  pl.BlockSpec((tk, tn), lambda i, j, k: (k, j))],
        out_specs=pl.BlockSpec((tm, tn), lambda i, j, k: (i, j)),
        scratch_shapes=[pltpu.VMEM((tm, tn), F32)],
        compiler_params=_params("parallel", "parallel", "arbitrary"),
        name="dft_matmul",
    )(a, b)


def _dft_half_kernel(c_ref, s_ref, a_ref, b_ref, direct_ref, mirrored_ref, accp_ref, accq_ref):
    k = pl.program_id(2)

    @pl.when(k == 0)
    def _():
        accp_ref[...] = jnp.zeros_like(accp_ref)
        accq_ref[...] = jnp.zeros_like(accq_ref)

    accp_ref[...] += _dot(c_ref[...], a_ref[0])
    accq_ref[...] += _dot(s_ref[...], b_ref[0])

    @pl.when(k == pl.num_programs(2) - 1)
    def _():
        direct_ref[...] = (accp_ref[...] + accq_ref[...]).astype(direct_ref.dtype)
        mirrored_ref[...] = (accp_ref[...] - accq_ref[...]).astype(mirrored_ref.dtype)


def _dft_half(g_top, ab, tm, tn, tk):
    h, n2 = g_top.shape
    n = n2 // 2
    cols = ab.shape[2]
    out = jax.ShapeDtypeStruct((h, cols), BF16)
    kb = n // tk
    return pl.pallas_call(
        _dft_half_kernel,
        out_shape=(out, out),
        grid=(h // tm, cols // tn, kb),
        in_specs=[pl.BlockSpec((tm, tk), lambda i, j, k: (i, k)),
                  pl.BlockSpec((tm, tk), lambda i, j, k: (i, k + kb)),
                  pl.BlockSpec((1, tk, tn), lambda i, j, k: (0, k, j)),
                  pl.BlockSpec((1, tk, tn), lambda i, j, k: (1, k, j))],
        out_specs=(pl.BlockSpec((tm, tn), lambda i, j, k: (i, j)),
                   pl.BlockSpec((tm, tn), lambda i, j, k: (i, j))),
        scratch_shapes=[pltpu.VMEM((tm, tn), F32), pltpu.VMEM((tm, tn), F32)],
        compiler_params=_params("parallel", "parallel", "arbitrary"),
        name="dft_half",
    )(g_top, g_top, ab, ab)


def _proj_kernel(x_ref, sc_ref, sh_ref, g_ref, win_ref, cw_ref, qg_ref, wq_ref, kvg_ref, wk_ref, wvt_ref,
                 cosq_ref, sinq_ref, cosk_ref, sink_ref, ab_ref, cp_ref, q_ref, k_ref, vt_ref):
    gain = g_ref[...] * (1.0 + sc_ref[0])
    shift = sh_ref[0]
    half = MLA_HEADS * HEAD_PAD
    t = x_ref.shape[1]
    sub = min(t, PROJ_SUB_ROWS)
    for r0 in range(0, t, sub):
        rows = slice(r0, r0 + sub)
        hb = (_rms(x_ref[0, rows]) * gain + shift).astype(BF16)

        cq = _dot(hb, win_ref[:, OFF_MLA_Q:OFF_MLA_KV])
        ckv = _dot(hb, win_ref[:, OFF_MLA_KV:OFF_MLA_KR])
        u = _dot(hb, win_ref[:, :OFF_CONV])
        kp2 = _dot(hb, win_ref[:, OFF_MLA_KR:EXT_COLS])
        cp_ref[0, rows] = _dot(hb, win_ref[:, OFF_CONV:OFF_MLA_Q])

        cqn = (_rms(cq) * qg_ref[...]).astype(BF16)
        qq = _dot(cqn, wq_ref[...])
        cosq = cosq_ref[rows]
        sinq = sinq_ref[rows]
        for hd in range(MLA_HEADS):
            lo = hd * HEAD_PAD
            q_ref[0, rows, lo:lo + HEAD_PAD] = (qq[:, lo:lo + HEAD_PAD] * cosq
                                                 + qq[:, half + lo:half + lo + HEAD_PAD] * sinq).astype(BF16)

        ab = _dot(u.astype(BF16), cw_ref[...])
        ab_ref[0, rows] = ab[:, :GROUP_W].astype(BF16)
        ab_ref[1, rows] = ab[:, GROUP_W:].astype(BF16)

        ckvn = (_rms(ckv) * kvg_ref[...]).astype(BF16)
        kn = _dot(ckvn, wk_ref[...])
        kp = kp2[:, :LANES] * cosk_ref[rows] + kp2[:, LANES:] * sink_ref[rows]
        for hd in range(MLA_HEADS):
            lo = hd * HEAD_PAD
            k_ref[0, rows, lo:lo + HEAD_PAD] = (kn[:, lo:lo + HEAD_PAD] + kp).astype(BF16)
        vt_ref[0, :, rows] = lax.dot_general(wvt_ref[...], ckvn, _NT, preferred_element_type=F32).astype(BF16)


def _proj(x, scale, shift, g, win, cw, qg, wq, kvg, wk, wvt, cosq, sinq, cosk, sink, t):
    b, n, d = x.shape
    row = lambda bi, i: (bi, 0, 0)
    tile = lambda bi, i: (bi, i, 0)
    tab = pl.BlockSpec((t, LANES), lambda bi, i: (i, 0))
    qk_w = MLA_HEADS * HEAD_PAD
    return pl.pallas_call(
        _proj_kernel,
        out_shape=(jax.ShapeDtypeStruct((2, n, b * GROUP_W), BF16),
                   jax.ShapeDtypeStruct((b, n, 4 * GROUP_W), F32),
                   jax.ShapeDtypeStruct((b, n, qk_w), BF16),
                   jax.ShapeDtypeStruct((b, n, qk_w), BF16),
                   jax.ShapeDtypeStruct((b, GROUP_W, n), BF16)),
        grid=(b, n // t),
        in_specs=[pl.BlockSpec((1, t, d), tile),
                  pl.BlockSpec((1, 1, d), row),
                  pl.BlockSpec((1, 1, d), row),
                  _const_spec(g.shape), _const_spec(win.shape), _const_spec(cw.shape),
                  _const_spec(qg.shape), _const_spec(wq.shape), _const_spec(kvg.shape),
                  _const_spec(wk.shape), _const_spec(wvt.shape), tab, tab, tab, tab],
        out_specs=(pl.BlockSpec((2, t, GROUP_W), lambda bi, i: (0, i, bi)),
                   pl.BlockSpec((1, t, 4 * GROUP_W), tile),
                   pl.BlockSpec((1, t, qk_w), tile),
                   pl.BlockSpec((1, t, qk_w), tile),
                   pl.BlockSpec((1, GROUP_W, t), lambda bi, i: (bi, 0, i))),
        compiler_params=_params("parallel", "parallel"),
        name="proj",
    )(x, scale, shift, g, win, cw, qg, wq, kvg, wk, wvt, cosq, sinq, cosk, sink)


def _unit_scores(q_ref, k_refs, heads, s_ref, m_ref):
    for i, hd in enumerate(heads):
        lo = hd * HEAD_PAD
        qh = q_ref[0, :, lo:lo + HEAD_PAD]
        off = 0
        m = None
        for k_ref in k_refs:
            nk = k_ref.shape[1]
            s = lax.dot_general(k_ref[0, :, lo:lo + HEAD_PAD], qh, _NT, preferred_element_type=F32)
            s_ref[i, off:off + nk] = s
            mj = jnp.max(s, axis=0, keepdims=True)
            m = mj if m is None else jnp.maximum(m, mj)
            off += nk
        m_ref[i] = m


def _unit_softmax_pv(s_ref, m_ref, vt_refs, heads, o_ref):
    outs = []
    for i, hd in enumerate(heads):
        m = m_ref[i]
        l = None
        o = None
        off = 0
        for vt_ref in vt_refs:
            nk = vt_ref.shape[2]
            p = jnp.exp2(s_ref[i, off:off + nk] - m)
            lj = jnp.sum(p, axis=0, keepdims=True)
            oj = _dot(vt_ref[0, hd * MLA_V:(hd + 1) * MLA_V, :], p.astype(BF16))
            l = lj if l is None else l + lj
            o = oj if o is None else o + oj
            off += nk
        outs.append(o / l)
    o_ref[0] = jnp.concatenate(outs, axis=0).T.astype(o_ref.dtype)


def _unit_scratch(n_heads, n_keys, tq):
    return [pltpu.VMEM((n_heads, n_keys, tq), F32), pltpu.VMEM((n_heads, 1, tq), F32)]


def _attn_kernel(n_sets, q_ref, *refs):
    o_ref, s_ref, m_ref = refs[2 * n_sets:]
    heads = tuple(range(MLA_HEADS))
    _unit_scores(q_ref, refs[:n_sets], heads, s_ref, m_ref)
    _unit_softmax_pv(s_ref, m_ref, refs[n_sets:2 * n_sets], heads, o_ref)


def _attn(q, kv_sets, tq):
    b, n, qk_w = q.shape
    whole = lambda bi, i: (bi, 0, 0)
    ks = [k for k, _ in kv_sets]
    vts = [vt for _, vt in kv_sets]
    return pl.pallas_call(
        functools.partial(_attn_kernel, len(kv_sets)),
        out_shape=jax.ShapeDtypeStruct((b, n, GROUP_W), BF16),
        grid=(b, n // tq),
        in_specs=([pl.BlockSpec((1, tq, qk_w), lambda bi, i: (bi, i, 0))]
                  + [pl.BlockSpec((1,) + k.shape[1:], whole) for k in ks]
                  + [pl.BlockSpec((1,) + vt.shape[1:], whole) for vt in vts]),
        out_specs=pl.BlockSpec((1, tq, GROUP_W), lambda bi, i: (bi, i, 0)),
        scratch_shapes=_unit_scratch(MLA_HEADS, sum(k.shape[1] for k in ks), tq),
        compiler_params=_params("parallel", "arbitrary"),
        name="attn",
    )(q, *ks, *vts)


def _attn_pipe_kernel(n_sets, qe_ref, qo_ref, *refs):
    k_refs = refs[:n_sets]
    vt_prev = refs[n_sets:2 * n_sets]
    vt_cur = refs[2 * n_sets:3 * n_sets]
    oe_ref, oo_ref, s0, m0, s1, m1 = refs[3 * n_sets:]
    heads = tuple(range(MLA_HEADS))

    @pl.when(pl.program_id(0) == 0)
    def _():
        s1[...] = jnp.zeros_like(s1)
        m1[...] = jnp.zeros_like(m1)

    _unit_scores(qe_ref, k_refs, heads, s0, m0)
    _unit_softmax_pv(s1, m1, vt_prev, heads, oo_ref)
    _unit_scores(qo_ref, k_refs, heads, s1, m1)
    _unit_softmax_pv(s0, m0, vt_cur, heads, oe_ref)


def _attn_pipelined(q, kv_sets, tq):
    b, n, qk_w = q.shape
    nq = n // tq
    pairs = b * nq // 2
    ks = [k for k, _ in kv_sets]
    vts = [vt for _, vt in kv_sets]
    even = lambda j: jnp.minimum(2 * j, 2 * pairs - 2)
    b_cur = lambda j: even(j) // nq
    b_prev = lambda j: jnp.maximum(2 * j - 1, 0) // nq
    cur = lambda j: (jnp.minimum(j, pairs - 1), 0, 0)
    prev = lambda j: (jnp.maximum(j - 1, 0), 0, 0)
    once = pl.Buffered(1)
    out = jax.ShapeDtypeStruct((pairs, tq, GROUP_W), BF16)
    out_block = (1, tq, GROUP_W)
    n_keys = sum(k.shape[1] for k in ks)
    even_tiles, odd_tiles = pl.pallas_call(
        functools.partial(_attn_pipe_kernel, len(kv_sets)),
        out_shape=(out, out),
        grid=(pairs + 1,),
        in_specs=([pl.BlockSpec((1, tq, qk_w), lambda j: (b_cur(j), even(j) % nq, 0)),
                   pl.BlockSpec((1, tq, qk_w), lambda j: (b_cur(j), even(j) % nq + 1, 0))]
                  + [pl.BlockSpec((1,) + k.shape[1:], lambda j: (b_cur(j), 0, 0)) for k in ks]
                  + [pl.BlockSpec((1,) + vt.shape[1:], lambda j: (b_prev(j), 0, 0), pipeline_mode=once)
                     for vt in vts]
                  + [pl.BlockSpec((1,) + vt.shape[1:], lambda j: (b_cur(j), 0, 0)) for vt in vts]),
        out_specs=(pl.BlockSpec(out_block, cur), pl.BlockSpec(out_block, prev)),
        scratch_shapes=_unit_scratch(MLA_HEADS, n_keys, tq) * 2,
        compiler_params=pltpu.CompilerParams(dimension_semantics=("arbitrary",),
                                             vmem_limit_bytes=ATTN_VMEM_LIMIT),
        name="attn_pipe",
    )(q, q, *ks, *vts, *vts)
    return [[even_tiles], [odd_tiles]]


def _fourier_tile(i, n_tiles, f_refs):
    if len(f_refs) == 1:
        return f_refs[0][...]
    direct_ref, mirrored_ref, mrow_ref, middle_ref, flip_ref = f_refs
    t = direct_ref.shape[0]
    flipped = _dot(flip_ref[...], mirrored_ref[...]).astype(BF16)
    row0 = jnp.where(2 * i == n_tiles, middle_ref[0:1, :], mrow_ref[0:1, :])
    is_row0 = lax.broadcasted_iota(jnp.int32, (t, GROUP_W), 0) == 0
    mirrored = jnp.where(is_row0, row0, flipped)
    return jnp.where(2 * i >= n_tiles, mirrored, direct_ref[...])


def _mixmlp_kernel(n_tokens, final, n_f, x_ref, cp_ref, prev_ref, next_ref, g1_ref, sc2_ref,
                   sh2_ref, g2_ref, convw_ref, poolw_ref, pscale_ref, wout_ref, n2g_ref, w1_ref, w2_ref,
                   fg_ref, *tail):
    f_refs, at_refs, o_ref = tail[:n_f], tail[n_f:-1], tail[-1]
    i = pl.program_id(1)
    t = x_ref.shape[1]
    first = i == 0
    last = i == pl.num_programs(1) - 1
    prev = jnp.where(first, 0.0, prev_ref[0])
    nxt = jnp.where(last, 0.0, next_ref[0])
    cpe = jnp.concatenate([prev, cp_ref[0], nxt], axis=0)
    ext = t + 2 * HALO
    inner = slice(HALO, HALO + t)
    lane = lax.broadcasted_iota(jnp.int32, (t, LANES), 1)
    tok = lax.broadcasted_iota(jnp.int32, (t, LANES), 0) + i * t
    lo_half = lane < SUB_W

    def shifted(a, k):
        return pltpu.roll(a, k % ext, axis=0)

    z = cpe[:, GROUP_W:2 * GROUP_W] * cpe[:, 2 * GROUP_W:3 * GROUP_W]
    y = shifted(z, 1) * convw_ref[0:1, :] + z * convw_ref[1:2, :] + shifted(z, -1) * convw_ref[2:3, :]
    conv = (cpe[:, :GROUP_W] * y)[inner]

    u = cpe[:, 3 * GROUP_W:]
    s2 = shifted(u, 1) + u
    s4 = shifted(s2, 1) + shifted(s2, -1)
    ub = u[:, LANES:]
    s4b = s4[:, LANES:]
    s8b = shifted(s4b, 2) + shifted(s4b, -2)
    s16b = shifted(s8b, 4) + shifted(s8b, -4)

    def mean(sum_lo, sum_hi, w_lo, w_hi):
        wh = jnp.where(lo_half, w_lo // 2, w_hi // 2)
        cnt = jnp.minimum(tok + wh - 1, n_tokens - 1) - jnp.maximum(tok - wh, 0) + 1
        return jnp.where(lo_half, sum_lo[inner], sum_hi[inner]) / cnt.astype(F32)

    pa = mean(s2[:, :LANES], s4[:, :LANES], POOL_WINDOWS[0], POOL_WINDOWS[1]) - u[inner, :LANES]
    pb = mean(s8b, s16b, POOL_WINDOWS[2], POOL_WINDOWS[3]) - ub[inner]
    pin = jnp.concatenate([pa, pb], axis=-1).astype(BF16)
    pool = _dot(pin, poolw_ref[...]) * pscale_ref[...]

    per_row = GROUP_W // at_refs[0].shape[2]
    attn = jnp.concatenate([jnp.concatenate([r[0] for r in at_refs[k:k + per_row]], axis=-1)
                            for k in range(0, len(at_refs), per_row)], axis=0)
    fourier = _fourier_tile(i, pl.num_programs(1), f_refs)
    mix = jnp.concatenate([fourier, conv.astype(BF16), pool.astype(BF16), attn], axis=-1)
    x1 = x_ref[0] + g1_ref[0] * _dot(mix, wout_ref[...])

    h2 = (_rms(x1) * (n2g_ref[...] * (1.0 + sc2_ref[0])) + sh2_ref[0]).astype(BF16)
    chunk = D_FF // MLP_CHUNKS
    acc = None
    for c in range(MLP_CHUNKS):
        hid = jnp.maximum(_dot(h2, w1_ref[:, c * chunk:(c + 1) * chunk]), 0.0)
        part = _dot((hid * hid).astype(BF16), w2_ref[c * chunk:(c + 1) * chunk, :])
        acc = part if acc is None else acc + part
    x2 = x1 + g2_ref[0] * acc
    if final:
        x2 = _rms(x2) * fg_ref[...]
    o_ref[0] = x2


def _mixmlp(x, cp, f, ats, g1, sc2, sh2, g2, convw, poolw, pscale, wout, n2g, w1, w2, fg, t, final):
    b, n, d = x.shape
    if isinstance(f, tuple):
        direct, mirrored, middle = f
        nt, nh, hb8 = n // t, n // t // 2, t // SUBLANES
        r = jnp.arange(t, dtype=jnp.int32)
        flip = ((r[:, None] + r[None, :]) == t).astype(BF16)
        fs = [direct, mirrored, mirrored, middle, flip]
        f_specs = [pl.BlockSpec((t, GROUP_W), lambda bi, i: (jnp.minimum(i, nh - 1), bi)),
                   pl.BlockSpec((t, GROUP_W), lambda bi, i: (jnp.clip(nt - 1 - i, 0, nh - 1), bi)),
                   pl.BlockSpec((SUBLANES, GROUP_W), lambda bi, i: (jnp.clip((nt - i) * hb8, 0, nh * hb8 - 1), bi)),
                   pl.BlockSpec((SUBLANES, GROUP_W), lambda bi, i: (0, bi)),
                   _const_spec(flip.shape)]
    else:
        fs = [f]
        f_specs = [pl.BlockSpec((t, GROUP_W), lambda bi, i: (i, bi))]
    at_specs = [pl.BlockSpec((1,) + a.shape[1:], lambda bi, i: (bi * (n // t) + i, 0, 0)) for r in ats for a in r]
    ats = [a for r in ats for a in r]
    hb = t // HALO
    nhb = n // HALO
    row = lambda bi, i: (bi, 0, 0)
    tile = lambda bi, i: (bi, i, 0)
    row_spec = pl.BlockSpec((1, 1, d), row)
    return pl.pallas_call(
        functools.partial(_mixmlp_kernel, n, final, len(fs)),
        out_shape=jax.ShapeDtypeStruct((b, n, d), F32),
        grid=(b, n // t),
        in_specs=[pl.BlockSpec((1, t, d), tile),
                  pl.BlockSpec((1, t, 4 * GROUP_W), tile),
                  pl.BlockSpec((1, HALO, 4 * GROUP_W), lambda bi, i: (bi, jnp.maximum(i * hb - 1, 0), 0)),
                  pl.BlockSpec((1, HALO, 4 * GROUP_W), lambda bi, i: (bi, jnp.minimum((i + 1) * hb, nhb - 1), 0)),
                  row_spec, row_spec, row_spec, row_spec,
                  _const_spec(convw.shape), _const_spec(poolw.shape), _const_spec(pscale.shape),
                  _const_spec(wout.shape), _const_spec(n2g.shape), _const_spec(w1.shape),
                  _const_spec(w2.shape), _const_spec(fg.shape)] + f_specs + at_specs,
        out_specs=pl.BlockSpec((1, t, d), tile),
        compiler_params=_params("parallel", "arbitrary"),
        name="mixmlp",
    )(x, cp, cp, cp, g1, sc2, sh2, g2, convw, poolw, pscale, wout, n2g, w1, w2, fg, *fs, *ats)


def _rope_rotate_cols(w):
    q = MLA_ROPE // 4
    return jnp.concatenate([-w[..., q:2 * q], w[..., :q], -w[..., 3 * q:], w[..., 2 * q:3 * q]], axis=-1)


def _layout_weights(w_in, w_uq, w_ukv, pool_w):
    d = w_in.shape[0]
    kr = w_in[:, OFF_MLA_KR:]
    z = lambda r, c: jnp.zeros((r, c), F32)
    pad_lo, pad_hi = MLA_NOPE, HEAD_PAD - MLA_NOPE - MLA_ROPE
    win = jnp.concatenate([w_in[:, :OFF_MLA_KR],
                           z(d, pad_lo), kr, z(d, pad_hi),
                           z(d, pad_lo), _rope_rotate_cols(kr), z(d, pad_hi)], axis=1)
    qh = w_uq.reshape(MLA_Q_RANK, MLA_HEADS, MLA_NOPE + MLA_ROPE)
    q_plain = jnp.pad(qh, ((0, 0), (0, 0), (0, pad_hi)))
    q_rot = jnp.pad(_rope_rotate_cols(qh[..., MLA_NOPE:]), ((0, 0), (0, 0), (pad_lo, pad_hi)))
    wq = jnp.concatenate([q_plain.reshape(MLA_Q_RANK, -1), q_rot.reshape(MLA_Q_RANK, -1)], axis=1)
    kvh = w_ukv.reshape(MLA_KV_RANK, MLA_HEADS, MLA_NOPE + MLA_V)
    k_plain = jnp.pad(kvh[..., :MLA_NOPE], ((0, 0), (0, 0), (0, HEAD_PAD - MLA_NOPE)))
    wk = k_plain.reshape(MLA_KV_RANK, -1)
    wvt = kvh[..., MLA_NOPE:].reshape(MLA_KV_RANK, -1).T
    poolw = jax.scipy.linalg.block_diag(*[pool_w[g] for g in range(N_SUB)])
    return win.astype(BF16), wq.astype(BF16), wk.astype(BF16), wvt.astype(BF16), poolw.astype(BF16)


def _rope_tables(n):
    rows = n // GRID_W
    row = jnp.repeat(jnp.arange(rows, dtype=F32), GRID_W)
    col = jnp.tile(jnp.arange(GRID_W, dtype=F32), rows)
    half = MLA_ROPE // 2
    inv = ROPE_BASE ** (-jnp.arange(0, half, 2, dtype=F32) / half)
    ang_r = row[:, None] * inv[None, :]
    ang_c = col[:, None] * inv[None, :]
    ang = jnp.concatenate([ang_r, ang_r, ang_c, ang_c], axis=-1)
    return _place_tables(jnp.cos(ang), jnp.sin(ang))


def _place_tables(cos, sin):
    n = cos.shape[0]
    qs = MLA_SCALE * LOG2E
    pad_hi = HEAD_PAD - MLA_NOPE - MLA_ROPE
    zl, zh = jnp.zeros((n, MLA_NOPE), F32), jnp.zeros((n, pad_hi), F32)
    cosq = jnp.concatenate([jnp.full((n, MLA_NOPE), qs, F32), cos * qs, zh], axis=1)
    sinq = jnp.concatenate([zl, sin * qs, zh], axis=1)
    cosk = jnp.concatenate([zl, cos, zh], axis=1)
    sink = jnp.concatenate([zl, sin, zh], axis=1)
    return cosq, sinq, cosk, sink


def _tile(n, want):
    return min(n, want)


def kernel(x, c, ctx, c_ctx, ada_w, ada_b, norm1_g, norm2_g, w_in, fourier_w, conv_w, pool_w, pool_scale,
           q_norm_g, w_uq, kv_norm_g, w_ukv, w_out, mlp_w1, mlp_w2, final_norm_g):
    b, n, d = x.shape
    nc = ctx.shape[1]
    depth = ada_w.shape[0]

    rows = -(-(b + 1) // SUBLANES) * SUBLANES
    cc = jnp.concatenate([c, c_ctx[None, :], jnp.zeros((rows - b - 1, d), F32)], axis=0)
    mod = _ada(cc, ada_w, ada_b)
    cw = _fold_fourier(fourier_w)
    half_dft = n >= 2 * TOKEN_TILE
    g_x = _dft_matrix(n, n // 2 if half_dft else n)
    g_c = _dft_matrix(nc, nc)
    sign = 1.0 - 2.0 * (jnp.arange(n) % 2).astype(F32)
    alt_row = jnp.zeros((SUBLANES, n), F32).at[0].set(sign / math.sqrt(n)).astype(BF16)
    tabs_x = _rope_tables(n)
    tabs_c = _place_tables(jnp.ones((nc, MLA_ROPE), F32), jnp.zeros((nc, MLA_ROPE), F32))
    fg = final_norm_g.reshape(1, d)

    def stream_mod(l, ctx_stream):
        m = mod[l, b:b + 1] if ctx_stream else mod[l, :b]
        m = jnp.broadcast_to(m, (b, 6 * d)).reshape(b, 1, 6, d)
        return [m[:, :, j, :] for j in range(6)]

    for l in range(depth):
        last = l == depth - 1
        win, wq, wk, wvt, poolw = _layout_weights(w_in[l], w_uq[l], w_ukv[l], pool_w[l])
        n1g = norm1_g[l].reshape(1, d)
        n2g = norm2_g[l].reshape(1, d)
        qg = q_norm_g[l].reshape(1, -1)
        kvg = kv_norm_g[l].reshape(1, -1)
        pscale = pool_scale[l].reshape(1, -1)
        wout = w_out[l].astype(BF16)
        w1 = mlp_w1[l].astype(BF16)
        w2 = mlp_w2[l].astype(BF16)

        def project(tokens, ctx_stream, tabs):
            sh1, sc1 = stream_mod(l, ctx_stream)[:2]
            t = _tile(tokens.shape[1], TOKEN_TILE)
            return _proj(tokens, sc1, sh1, n1g, win, cw[l], qg, wq, kvg, wk, wvt, *tabs, t)

        def finish(tokens, ctx_stream, cp, ab, g, q, kv_sets, final):
            nt = tokens.shape[1]
            t = _tile(nt, TOKEN_TILE)
            _, _, g1, sh2, sc2, g2 = stream_mod(l, ctx_stream)
            cols = b * GROUP_W
            if g.shape[0] < nt:
                direct, mirrored = _dft_half(g, ab, _tile(nt // 2, 1024), _tile(cols, 1024), _tile(nt, 2048))
                middle = _matmul(alt_row, ab[0], SUBLANES, _tile(cols, 2048), _tile(nt, 1024))
                f = (direct, mirrored, middle)
            else:
                f = _matmul(g, ab.reshape(2 * nt, cols), _tile(nt, 1024), _tile(cols, 2048), _tile(2 * nt, 1024))
            if t == 2 * ATTN_TILE:
                ats = _attn_pipelined(q, kv_sets, ATTN_TILE)
            else:
                ats = [[_attn(q, kv_sets, t).reshape(b * nt // t, t, GROUP_W)]]
            return _mixmlp(tokens, cp, f, ats, g1, sc2, sh2, g2, conv_w[l], poolw, pscale, wout, n2g, w1, w2,
                           fg, t, final)

        ab_c, cp_c, q_c, k_c, vt_c = project(ctx, True, tabs_c)
        ab_x, cp_x, q_x, k_x, vt_x = project(x, False, tabs_x)
        x = finish(x, False, cp_x, ab_x, g_x, q_x, [(k_x, vt_x), (k_c, vt_c)], last)
        if not last:
            ctx = finish(ctx, True, cp_c, ab_c, g_c, q_c, [(k_c, vt_c)], False)
    return x
```

```python
import functools
import math

import jax
import jax.numpy as jnp
from jax import lax
from jax.experimental import pallas as pl
from jax.experimental.pallas import tpu as pltpu

D_MODEL = 1024
DEPTH = 2
GRID_W = 64
GROUP_W = D_MODEL // 4
N_SUB = 4
SUB_W = GROUP_W // N_SUB
POOL_WINDOWS = (2, 4, 8, 16)
MLA_HEADS = 4
MLA_NOPE = GROUP_W // MLA_HEADS
MLA_ROPE = MLA_NOPE // 2
MLA_V = GROUP_W // MLA_HEADS
MLA_Q_RANK = GROUP_W
MLA_KV_RANK = GROUP_W // 2
MLA_SCALE = 1.0 / math.sqrt(MLA_NOPE + MLA_ROPE)
ROPE_BASE = 10000.0
D_FF = 4 * D_MODEL
EPS = 1e-6

OFF_CONV = GROUP_W
OFF_POOL = OFF_CONV + 3 * GROUP_W
OFF_MLA_Q = OFF_POOL + GROUP_W
OFF_MLA_KV = OFF_MLA_Q + MLA_Q_RANK
OFF_MLA_KR = OFF_MLA_KV + MLA_KV_RANK
IN_COLS = OFF_MLA_KR + MLA_ROPE

LANES = 128
SUBLANES = 8
HEAD_PAD = LANES
EXT_COLS = OFF_MLA_KR + 2 * LANES
DFT_ROWS = 64
HALO = SUBLANES
MLP_CHUNKS = 4
PROJ_SUB_ROWS = 256
VMEM_LIMIT = 52 * 1024 * 1024
ATTN_VMEM_LIMIT = 58 * 1024 * 1024
ATTN_TILE = 256
TOKEN_TILE = 512
PROJ_TILE = 1024
LOG2E = math.log2(math.e)

F32 = jnp.float32
BF16 = jnp.bfloat16


_NT = (((1,), (1,)), ((), ()))


def _dot(a, b):
    return jnp.dot(a, b, preferred_element_type=F32)


def _rms(x):
    return x * lax.rsqrt(jnp.mean(x * x, axis=-1, keepdims=True) + EPS)


def _params(*sem):
    return pltpu.CompilerParams(dimension_semantics=sem, vmem_limit_bytes=VMEM_LIMIT)


def _const_spec(shape):
    nd = len(shape)
    return pl.BlockSpec(shape, lambda *_: (0,) * nd, pipeline_mode=pl.Buffered(1))


def _ada_kernel(cc_ref, w_ref, b_ref, o_ref):
    cc = cc_ref[...]
    s = cc * (1.0 / (1.0 + jnp.exp(-cc)))
    o_ref[0] = _dot(s.astype(BF16), w_ref[0].astype(BF16)) + b_ref[0]


def _ada(cc, ada_w, ada_b):
    depth, d, cols = ada_w.shape
    rows = cc.shape[0]
    tn = 1536
    return pl.pallas_call(
        _ada_kernel,
        out_shape=jax.ShapeDtypeStruct((depth, rows, cols), F32),
        grid=(depth, cols // tn),
        in_specs=[pl.BlockSpec((rows, d), lambda l, j: (0, 0)),
                  pl.BlockSpec((1, d, tn), lambda l, j: (l, 0, j)),
                  pl.BlockSpec((1, 1, tn), lambda l, j: (l, 0, j))],
        out_specs=pl.BlockSpec((1, rows, tn), lambda l, j: (l, 0, j)),
        compiler_params=_params("arbitrary", "arbitrary"),
        name="ada",
    )(cc, ada_w, ada_b.reshape(depth, 1, cols))


def _fold_kernel(dft_ref, w_ref, o_ref):
    w = w_ref[0]
    c = jnp.dot(dft_ref[:, :GROUP_W], w, preferred_element_type=F32, precision=lax.Precision.HIGHEST)
    s = jnp.dot(dft_ref[:, GROUP_W:], w, preferred_element_type=F32, precision=lax.Precision.HIGHEST)
    o_ref[0, :, :GROUP_W] = c.astype(BF16)
    o_ref[0, :, GROUP_W:] = s.astype(BF16)


def _channel_dft():
    i = jnp.arange(GROUP_W, dtype=jnp.int32)
    same = (i[:, None] // SUB_W) == (i[None, :] // SUB_W)
    ang = ((i[:, None] % SUB_W) * (i[None, :] % SUB_W) % SUB_W).astype(F32) * (2.0 * math.pi / SUB_W)
    scale = 1.0 / math.sqrt(SUB_W)
    c = jnp.where(same, jnp.cos(ang), 0.0) * scale
    s = jnp.where(same, jnp.sin(ang), 0.0) * scale
    return jnp.concatenate([c, s], axis=1)


def _fold_fourier(fourier_w):
    depth = fourier_w.shape[0]
    return pl.pallas_call(
        _fold_kernel,
        out_shape=jax.ShapeDtypeStruct((depth, GROUP_W, 2 * GROUP_W), BF16),
        grid=(depth,),
        in_specs=[pl.BlockSpec((GROUP_W, 2 * GROUP_W), lambda l: (0, 0)),
                  pl.BlockSpec((1, GROUP_W, GROUP_W), lambda l: (l, 0, 0))],
        out_specs=pl.BlockSpec((1, GROUP_W, 2 * GROUP_W), lambda l: (l, 0, 0)),
        compiler_params=_params("arbitrary"),
        name="fold_fourier",
    )(_channel_dft(), fourier_w)


def _dftgen_kernel(n, c1_ref, s1_ref, cb_ref, sb_ref, o_ref):
    c1 = c1_ref[0]
    s1 = s1_ref[0]
    cb = cb_ref[...]
    sb = sb_ref[...]
    scale = 1.0 / math.sqrt(n)
    o_ref[:, :n] = ((c1 * cb - s1 * sb) * scale).astype(BF16)
    o_ref[:, n:] = ((s1 * cb + c1 * sb) * (-scale)).astype(BF16)


def _dft_matrix(n, rows):
    nb = rows // DFT_ROWS
    k = jnp.arange(n, dtype=jnp.int32)
    j1 = jnp.arange(nb, dtype=jnp.int32) * DFT_ROWS
    j0 = jnp.arange(DFT_ROWS, dtype=jnp.int32)
    w = 2.0 * math.pi / n
    a1 = ((j1[:, None] * k[None, :]) % n).astype(F32) * w
    a0 = ((j0[:, None] * k[None, :]) % n).astype(F32) * w
    row_spec = pl.BlockSpec((1, 1, n), lambda i: (i, 0, 0))
    tab_spec = pl.BlockSpec((DFT_ROWS, n), lambda i: (0, 0))
    return pl.pallas_call(
        functools.partial(_dftgen_kernel, n),
        out_shape=jax.ShapeDtypeStruct((rows, 2 * n), BF16),
        grid=(nb,),
        in_specs=[row_spec, row_spec, tab_spec, tab_spec],
        out_specs=pl.BlockSpec((DFT_ROWS, 2 * n), lambda i: (i, 0)),
        compiler_params=_params("arbitrary"),
        name="dft_matrix",
    )(jnp.cos(a1).reshape(nb, 1, n), jnp.sin(a1).reshape(nb, 1, n), jnp.cos(a0), jnp.sin(a0))


def _matmul_kernel(a_ref, b_ref, o_ref, acc_ref):
    k = pl.program_id(2)

    @pl.when(k == 0)
    def _():
        acc_ref[...] = jnp.zeros_like(acc_ref)

    acc_ref[...] += _dot(a_ref[...], b_ref[...])

    @pl.when(k == pl.num_programs(2) - 1)
    def _():
        o_ref[...] = acc_ref[...].astype(o_ref.dtype)


def _matmul(a, b, tm, tn, tk):
    m, kk = a.shape
    n = b.shape[1]
    return pl.pallas_call(
        _matmul_kernel,
        out_shape=jax.ShapeDtypeStruct((m, n), BF16),
        grid=(m // tm, n // tn, kk // tk),
        in_specs=[pl.BlockSpec((tm, tk), lambda i, j, k: (i, k)),
                  pl.BlockSpec((tk, tn), lambda i, j, k: (k, j))],
        out_specs=pl.BlockSpec((tm, tn), lambda i, j, k: (i, j)),
        scratch_shapes=[pltpu.VMEM((tm, tn), F32)],
        compiler_params=_params("parallel", "parallel", "arbitrary"),
        name="dft_matmul",
    )(a, b)


def _dft_half_kernel(c_ref, s_ref, a_ref, b_ref, direct_ref, mirrored_ref, accp_ref, accq_ref):
    k = pl.program_id(2)

    @pl.when(k == 0)
    def _():
        accp_ref[...] = jnp.zeros_like(accp_ref)
        accq_ref[...] = jnp.zeros_like(accq_ref)

    accp_ref[...] += _dot(c_ref[...], a_ref[0])
    accq_ref[...] += _dot(s_ref[...], b_ref[0])

    @pl.when(k == pl.num_programs(2) - 1)
    def _():
        direct_ref[...] = (accp_ref[...] + accq_ref[...]).astype(direct_ref.dtype)
        mirrored_ref[...] = (accp_ref[...] - accq_ref[...]).astype(mirrored_ref.dtype)


def _dft_half(g_top, ab, tm, tn, tk):
    h, n2 = g_top.shape
    n = n2 // 2
    cols = ab.shape[2]
    out = jax.ShapeDtypeStruct((h, cols), BF16)
    kb = n // tk
    return pl.pallas_call(
        _dft_half_kernel,
        out_shape=(out, out),
        grid=(h // tm, cols // tn, kb),
        in_specs=[pl.BlockSpec((tm, tk), lambda i, j, k: (i, k)),
                  pl.BlockSpec((tm, tk), lambda i, j, k: (i, k + kb)),
                  pl.BlockSpec((1, tk, tn), lambda i, j, k: (0, k, j)),
                  pl.BlockSpec((1, tk, tn), lambda i, j, k: (1, k, j))],
        out_specs=(pl.BlockSpec((tm, tn), lambda i, j, k: (i, j)),
                   pl.BlockSpec((tm, tn), lambda i, j, k: (i, j))),
        scratch_shapes=[pltpu.VMEM((tm, tn), F32), pltpu.VMEM((tm, tn), F32)],
        compiler_params=_params("parallel", "parallel", "arbitrary"),
        name="dft_half",
    )(g_top, g_top, ab, ab)


def _proj_kernel(x_ref, sc_ref, sh_ref, g_ref, win_ref, cw_ref, qg_ref, wq_ref, kvg_ref, wk_ref, wvt_ref,
                 cosq_ref, sinq_ref, cosk_ref, sink_ref, ab_ref, cp_ref, q_ref, k_ref, vt_ref):
    gain = g_ref[...] * (1.0 + sc_ref[0])
    shift = sh_ref[0]
    half = MLA_HEADS * HEAD_PAD
    t = x_ref.shape[1]
    sub = min(t, PROJ_SUB_ROWS)
    for r0 in range(0, t, sub):
        rows = slice(r0, r0 + sub)
        hb = (_rms(x_ref[0, rows]) * gain + shift).astype(BF16)

        cq = _dot(hb, win_ref[:, OFF_MLA_Q:OFF_MLA_KV])
        ckv = _dot(hb, win_ref[:, OFF_MLA_KV:OFF_MLA_KR])
        u = _dot(hb, win_ref[:, :OFF_CONV])
        kp2 = _dot(hb, win_ref[:, OFF_MLA_KR:EXT_COLS])
        cp_ref[0, rows] = _dot(hb, win_ref[:, OFF_CONV:OFF_MLA_Q])

        cqn = (_rms(cq) * qg_ref[...]).astype(BF16)
        qq = _dot(cqn, wq_ref[...])
        cosq = cosq_ref[rows]
        sinq = sinq_ref[rows]
        for hd in range(MLA_HEADS):
            lo = hd * HEAD_PAD
            q_ref[0, rows, lo:lo + HEAD_PAD] = (qq[:, lo:lo + HEAD_PAD] * cosq
                                                 + qq[:, half + lo:half + lo + HEAD_PAD] * sinq).astype(BF16)

        ab = _dot(u.astype(BF16), cw_ref[...])
        ab_ref[0, rows] = ab[:, :GROUP_W].astype(BF16)
        ab_ref[1, rows] = ab[:, GROUP_W:].astype(BF16)

        ckvn = (_rms(ckv) * kvg_ref[...]).astype(BF16)
        kn = _dot(ckvn, wk_ref[...])
        kp = kp2[:, :LANES] * cosk_ref[rows] + kp2[:, LANES:] * sink_ref[rows]
        for hd in range(MLA_HEADS):
            lo = hd * HEAD_PAD
            k_ref[0, rows, lo:lo + HEAD_PAD] = (kn[:, lo:lo + HEAD_PAD] + kp).astype(BF16)
        vt_ref[0, :, rows] = lax.dot_general(wvt_ref[...], ckvn, _NT, preferred_element_type=F32).astype(BF16)


def _proj(x, scale, shift, g, win, cw, qg, wq, kvg, wk, wvt, cosq, sinq, cosk, sink, t):
    b, n, d = x.shape
    row = lambda bi, i: (bi, 0, 0)
    tile = lambda bi, i: (bi, i, 0)
    tab = pl.BlockSpec((t, LANES), lambda bi, i: (i, 0))
    qk_w = MLA_HEADS * HEAD_PAD
    return pl.pallas_call(
        _proj_kernel,
        out_shape=(jax.ShapeDtypeStruct((2, n, b * GROUP_W), BF16),
                   jax.ShapeDtypeStruct((b, n, 4 * GROUP_W), F32),
                   jax.ShapeDtypeStruct((b, n, qk_w), BF16),
                   jax.ShapeDtypeStruct((b, n, qk_w), BF16),
                   jax.ShapeDtypeStruct((b, GROUP_W, n), BF16)),
        grid=(b, n // t),
        in_specs=[pl.BlockSpec((1, t, d), tile),
                  pl.BlockSpec((1, 1, d), row),
                  pl.BlockSpec((1, 1, d), row),
                  _const_spec(g.shape), _const_spec(win.shape), _const_spec(cw.shape),
                  _const_spec(qg.shape), _const_spec(wq.shape), _const_spec(kvg.shape),
                  _const_spec(wk.shape), _const_spec(wvt.shape), tab, tab, tab, tab],
        out_specs=(pl.BlockSpec((2, t, GROUP_W), lambda bi, i: (0, i, bi)),
                   pl.BlockSpec((1, t, 4 * GROUP_W), tile),
                   pl.BlockSpec((1, t, qk_w), tile),
                   pl.BlockSpec((1, t, qk_w), tile),
                   pl.BlockSpec((1, GROUP_W, t), lambda bi, i: (bi, 0, i))),
        compiler_params=_params("parallel", "parallel"),
        name="proj",
    )(x, scale, shift, g, win, cw, qg, wq, kvg, wk, wvt, cosq, sinq, cosk, sink)


def _head_scores(q_ref, k_refs, hd, s_ref, m_ref):
    lo = hd * HEAD_PAD
    qh = q_ref[0, :, lo:lo + HEAD_PAD]
    off = 0
    m = None
    for k_ref in k_refs:
        nk = k_ref.shape[1]
        s = lax.dot_general(k_ref[0, :, lo:lo + HEAD_PAD], qh, _NT, preferred_element_type=F32)
        s_ref[hd, off:off + nk] = s
        mj = jnp.max(s, axis=0, keepdims=True)
        m = mj if m is None else jnp.maximum(m, mj)
        off += nk
    m_ref[hd] = m


def _head_softmax_pv(s_ref, m_ref, vt_refs, hd):
    m = m_ref[hd]
    l = None
    o = None
    off = 0
    for vt_ref in vt_refs:
        nk = vt_ref.shape[2]
        p = jnp.exp2(s_ref[hd, off:off + nk] - m)
        lj = jnp.sum(p, axis=0, keepdims=True)
        oj = _dot(vt_ref[0, hd * MLA_V:(hd + 1) * MLA_V, :], p.astype(BF16))
        l = lj if l is None else l + lj
        o = oj if o is None else o + oj
        off += nk
    return o / l


def _store_heads(outs, o_ref):
    o_ref[0] = jnp.concatenate(outs, axis=0).T.astype(o_ref.dtype)


def _tile_scratch(n_keys, tq):
    return [pltpu.VMEM((MLA_HEADS, n_keys, tq), F32), pltpu.VMEM((MLA_HEADS, 1, tq), F32)]


def _attn_kernel(n_sets, q_ref, *refs):
    o_ref, s_ref, m_ref = refs[2 * n_sets:]
    for hd in range(MLA_HEADS):
        _head_scores(q_ref, refs[:n_sets], hd, s_ref, m_ref)
    _store_heads([_head_softmax_pv(s_ref, m_ref, refs[n_sets:2 * n_sets], hd) for hd in range(MLA_HEADS)], o_ref)


def _attn(q, kv_sets, tq):
    b, n, qk_w = q.shape
    whole = lambda bi, i: (bi, 0, 0)
    ks = [k for k, _ in kv_sets]
    vts = [vt for _, vt in kv_sets]
    return pl.pallas_call(
        functools.partial(_attn_kernel, len(kv_sets)),
        out_shape=jax.ShapeDtypeStruct((b, n, GROUP_W), BF16),
        grid=(b, n // tq),
        in_specs=([pl.BlockSpec((1, tq, qk_w), lambda bi, i: (bi, i, 0))]
                  + [pl.BlockSpec((1,) + k.shape[1:], whole) for k in ks]
                  + [pl.BlockSpec((1,) + vt.shape[1:], whole) for vt in vts]),
        out_specs=pl.BlockSpec((1, tq, GROUP_W), lambda bi, i: (bi, i, 0)),
        scratch_shapes=_tile_scratch(sum(k.shape[1] for k in ks), tq),
        compiler_params=_params("parallel", "arbitrary"),
        name="attn",
    )(q, *ks, *vts)


def _attn_pipe_kernel(n_sets, qe_ref, qo_ref, *refs):
    k_refs = refs[:n_sets]
    vt_prev = refs[n_sets:2 * n_sets]
    vt_cur = refs[2 * n_sets:3 * n_sets]
    oe_ref, oo_ref, s0, m0, s1, m1 = refs[3 * n_sets:]

    @pl.when(pl.program_id(0) == 0)
    def _():
        s1[...] = jnp.zeros_like(s1)
        m1[...] = jnp.zeros_like(m1)

    outs = []
    for hd in range(MLA_HEADS):
        _head_scores(qe_ref, k_refs, hd, s0, m0)
        outs.append(_head_softmax_pv(s1, m1, vt_prev, hd))
    _store_heads(outs, oo_ref)
    outs = []
    for hd in range(MLA_HEADS):
        _head_scores(qo_ref, k_refs, hd, s1, m1)
        outs.append(_head_softmax_pv(s0, m0, vt_cur, hd))
    _store_heads(outs, oe_ref)


def _attn_pipelined(q, kv_sets, tq):
    b, n, qk_w = q.shape
    nq = n // tq
    pairs = b * nq // 2
    ks = [k for k, _ in kv_sets]
    vts = [vt for _, vt in kv_sets]
    even = lambda j: jnp.minimum(2 * j, 2 * pairs - 2)
    b_cur = lambda j: even(j) // nq
    b_prev = lambda j: jnp.maximum(2 * j - 1, 0) // nq
    cur = lambda j: (jnp.minimum(j, pairs - 1), 0, 0)
    prev = lambda j: (jnp.maximum(j - 1, 0), 0, 0)
    once = pl.Buffered(1)
    out = jax.ShapeDtypeStruct((pairs, tq, GROUP_W), BF16)
    out_block = (1, tq, GROUP_W)
    n_keys = sum(k.shape[1] for k in ks)
    even_tiles, odd_tiles = pl.pallas_call(
        functools.partial(_attn_pipe_kernel, len(kv_sets)),
        out_shape=(out, out),
        grid=(pairs + 1,),
        in_specs=([pl.BlockSpec((1, tq, qk_w), lambda j: (b_cur(j), even(j) % nq, 0)),
                   pl.BlockSpec((1, tq, qk_w), lambda j: (b_cur(j), even(j) % nq + 1, 0))]
                  + [pl.BlockSpec((1,) + k.shape[1:], lambda j: (b_cur(j), 0, 0)) for k in ks]
                  + [pl.BlockSpec((1,) + vt.shape[1:], lambda j: (b_prev(j), 0, 0), pipeline_mode=once)
                     for vt in vts]
                  + [pl.BlockSpec((1,) + vt.shape[1:], lambda j: (b_cur(j), 0, 0)) for vt in vts]),
        out_specs=(pl.BlockSpec(out_block, cur), pl.BlockSpec(out_block, prev)),
        scratch_shapes=_tile_scratch(n_keys, tq) * 2,
        compiler_params=pltpu.CompilerParams(dimension_semantics=("arbitrary",),
                                             vmem_limit_bytes=ATTN_VMEM_LIMIT),
        name="attn_pipe",
    )(q, q, *ks, *vts, *vts)
    return [[even_tiles], [odd_tiles]]


def _fourier_tile(i, n_tiles, f_refs):
    if len(f_refs) == 1:
        return f_refs[0][...]
    direct_ref, mirrored_ref, mrow_ref, middle_ref, flip_ref = f_refs
    t = direct_ref.shape[0]
    flipped = _dot(flip_ref[...], mirrored_ref[...]).astype(BF16)
    row0 = jnp.where(2 * i == n_tiles, middle_ref[0:1, :], mrow_ref[0:1, :])
    is_row0 = lax.broadcasted_iota(jnp.int32, (t, GROUP_W), 0) == 0
    mirrored = jnp.where(is_row0, row0, flipped)
    return jnp.where(2 * i >= n_tiles, mirrored, direct_ref[...])


def _mixmlp_kernel(n_tokens, final, n_f, x_ref, cp_ref, prev_ref, next_ref, g1_ref, sc2_ref,
                   sh2_ref, g2_ref, convw_ref, poolw_ref, pscale_ref, wout_ref, n2g_ref, w1_ref, w2_ref,
                   fg_ref, *tail):
    f_refs, at_refs, o_ref = tail[:n_f], tail[n_f:-1], tail[-1]
    i = pl.program_id(1)
    t = x_ref.shape[1]
    first = i == 0
    last = i == pl.num_programs(1) - 1
    prev = jnp.where(first, 0.0, prev_ref[0])
    nxt = jnp.where(last, 0.0, next_ref[0])
    cpe = jnp.concatenate([prev, cp_ref[0], nxt], axis=0)
    ext = t + 2 * HALO
    inner = slice(HALO, HALO + t)
    lane = lax.broadcasted_iota(jnp.int32, (t, LANES), 1)
    tok = lax.broadcasted_iota(jnp.int32, (t, LANES), 0) + i * t
    lo_half = lane < SUB_W

    def shifted(a, k):
        return pltpu.roll(a, k % ext, axis=0)

    z = cpe[:, GROUP_W:2 * GROUP_W] * cpe[:, 2 * GROUP_W:3 * GROUP_W]
    y = shifted(z, 1) * convw_ref[0:1, :] + z * convw_ref[1:2, :] + shifted(z, -1) * convw_ref[2:3, :]
    conv = (cpe[:, :GROUP_W] * y)[inner]

    u = cpe[:, 3 * GROUP_W:]
    s2 = shifted(u, 1) + u
    s4 = shifted(s2, 1) + shifted(s2, -1)
    ub = u[:, LANES:]
    s4b = s4[:, LANES:]
    s8b = shifted(s4b, 2) + shifted(s4b, -2)
    s16b = shifted(s8b, 4) + shifted(s8b, -4)

    def mean(sum_lo, sum_hi, w_lo, w_hi):
        wh = jnp.where(lo_half, w_lo // 2, w_hi // 2)
        cnt = jnp.minimum(tok + wh - 1, n_tokens - 1) - jnp.maximum(tok - wh, 0) + 1
        return jnp.where(lo_half, sum_lo[inner], sum_hi[inner]) / cnt.astype(F32)

    pa = mean(s2[:, :LANES], s4[:, :LANES], POOL_WINDOWS[0], POOL_WINDOWS[1]) - u[inner, :LANES]
    pb = mean(s8b, s16b, POOL_WINDOWS[2], POOL_WINDOWS[3]) - ub[inner]
    pin = jnp.concatenate([pa, pb], axis=-1).astype(BF16)
    pool = _dot(pin, poolw_ref[...]) * pscale_ref[...]

    per_row = GROUP_W // at_refs[0].shape[2]
    attn = jnp.concatenate([jnp.concatenate([r[0] for r in at_refs[k:k + per_row]], axis=-1)
                            for k in range(0, len(at_refs), per_row)], axis=0)
    fourier = _fourier_tile(i, pl.num_programs(1), f_refs)
    mix = jnp.concatenate([fourier, conv.astype(BF16), pool.astype(BF16), attn], axis=-1)
    x1 = x_ref[0] + g1_ref[0] * _dot(mix, wout_ref[...])

    h2 = (_rms(x1) * (n2g_ref[...] * (1.0 + sc2_ref[0])) + sh2_ref[0]).astype(BF16)
    chunk = D_FF // MLP_CHUNKS
    acc = None
    for c in range(MLP_CHUNKS):
        hid = jnp.maximum(_dot(h2, w1_ref[:, c * chunk:(c + 1) * chunk]), 0.0)
        part = _dot((hid * hid).astype(BF16), w2_ref[c * chunk:(c + 1) * chunk, :])
        acc = part if acc is None else acc + part
    x2 = x1 + g2_ref[0] * acc
    if final:
        x2 = _rms(x2) * fg_ref[...]
    o_ref[0] = x2


def _mixmlp(x, cp, f, ats, g1, sc2, sh2, g2, convw, poolw, pscale, wout, n2g, w1, w2, fg, t, final):
    b, n, d = x.shape
    if isinstance(f, tuple):
        direct, mirrored, middle = f
        nt, nh, hb8 = n // t, n // t // 2, t // SUBLANES
        r = jnp.arange(t, dtype=jnp.int32)
        flip = ((r[:, None] + r[None, :]) == t).astype(BF16)
        fs = [direct, mirrored, mirrored, middle, flip]
        f_specs = [pl.BlockSpec((t, GROUP_W), lambda bi, i: (jnp.minimum(i, nh - 1), bi)),
                   pl.BlockSpec((t, GROUP_W), lambda bi, i: (jnp.clip(nt - 1 - i, 0, nh - 1), bi)),
                   pl.BlockSpec((SUBLANES, GROUP_W), lambda bi, i: (jnp.clip((nt - i) * hb8, 0, nh * hb8 - 1), bi)),
                   pl.BlockSpec((SUBLANES, GROUP_W), lambda bi, i: (0, bi)),
                   _const_spec(flip.shape)]
    else:
        fs = [f]
        f_specs = [pl.BlockSpec((t, GROUP_W), lambda bi, i: (i, bi))]
    at_specs = [pl.BlockSpec((1,) + a.shape[1:], lambda bi, i: (bi * (n // t) + i, 0, 0)) for r in ats for a in r]
    ats = [a for r in ats for a in r]
    hb = t // HALO
    nhb = n // HALO
    row = lambda bi, i: (bi, 0, 0)
    tile = lambda bi, i: (bi, i, 0)
    row_spec = pl.BlockSpec((1, 1, d), row)
    return pl.pallas_call(
        functools.partial(_mixmlp_kernel, n, final, len(fs)),
        out_shape=jax.ShapeDtypeStruct((b, n, d), F32),
        grid=(b, n // t),
        in_specs=[pl.BlockSpec((1, t, d), tile),
                  pl.BlockSpec((1, t, 4 * GROUP_W), tile),
                  pl.BlockSpec((1, HALO, 4 * GROUP_W), lambda bi, i: (bi, jnp.maximum(i * hb - 1, 0), 0)),
                  pl.BlockSpec((1, HALO, 4 * GROUP_W), lambda bi, i: (bi, jnp.minimum((i + 1) * hb, nhb - 1), 0)),
                  row_spec, row_spec, row_spec, row_spec,
                  _const_spec(convw.shape), _const_spec(poolw.shape), _const_spec(pscale.shape),
                  _const_spec(wout.shape), _const_spec(n2g.shape), _const_spec(w1.shape),
                  _const_spec(w2.shape), _const_spec(fg.shape)] + f_specs + at_specs,
        out_specs=pl.BlockSpec((1, t, d), tile),
        compiler_params=_params("parallel", "arbitrary"),
        name="mixmlp",
    )(x, cp, cp, cp, g1, sc2, sh2, g2, convw, poolw, pscale, wout, n2g, w1, w2, fg, *fs, *ats)


def _rope_rotate_cols(w):
    q = MLA_ROPE // 4
    return jnp.concatenate([-w[..., q:2 * q], w[..., :q], -w[..., 3 * q:], w[..., 2 * q:3 * q]], axis=-1)


def _layout_weights(w_in, w_uq, w_ukv, pool_w):
    d = w_in.shape[0]
    kr = w_in[:, OFF_MLA_KR:]
    z = lambda r, c: jnp.zeros((r, c), F32)
    pad_lo, pad_hi = MLA_NOPE, HEAD_PAD - MLA_NOPE - MLA_ROPE
    win = jnp.concatenate([w_in[:, :OFF_MLA_KR],
                           z(d, pad_lo), kr, z(d, pad_hi),
                           z(d, pad_lo), _rope_rotate_cols(kr), z(d, pad_hi)], axis=1)
    qh = w_uq.reshape(MLA_Q_RANK, MLA_HEADS, MLA_NOPE + MLA_ROPE)
    q_plain = jnp.pad(qh, ((0, 0), (0, 0), (0, pad_hi)))
    q_rot = jnp.pad(_rope_rotate_cols(qh[..., MLA_NOPE:]), ((0, 0), (0, 0), (pad_lo, pad_hi)))
    wq = jnp.concatenate([q_plain.reshape(MLA_Q_RANK, -1), q_rot.reshape(MLA_Q_RANK, -1)], axis=1)
    kvh = w_ukv.reshape(MLA_KV_RANK, MLA_HEADS, MLA_NOPE + MLA_V)
    k_plain = jnp.pad(kvh[..., :MLA_NOPE], ((0, 0), (0, 0), (0, HEAD_PAD - MLA_NOPE)))
    wk = k_plain.reshape(MLA_KV_RANK, -1)
    wvt = kvh[..., MLA_NOPE:].reshape(MLA_KV_RANK, -1).T
    poolw = jax.scipy.linalg.block_diag(*[pool_w[g] for g in range(N_SUB)])
    return win.astype(BF16), wq.astype(BF16), wk.astype(BF16), wvt.astype(BF16), poolw.astype(BF16)


def _rope_tables(n):
    rows = n // GRID_W
    row = jnp.repeat(jnp.arange(rows, dtype=F32), GRID_W)
    col = jnp.tile(jnp.arange(GRID_W, dtype=F32), rows)
    half = MLA_ROPE // 2
    inv = ROPE_BASE ** (-jnp.arange(0, half, 2, dtype=F32) / half)
    ang_r = row[:, None] * inv[None, :]
    ang_c = col[:, None] * inv[None, :]
    ang = jnp.concatenate([ang_r, ang_r, ang_c, ang_c], axis=-1)
    return _place_tables(jnp.cos(ang), jnp.sin(ang))


def _place_tables(cos, sin):
    n = cos.shape[0]
    qs = MLA_SCALE * LOG2E
    pad_hi = HEAD_PAD - MLA_NOPE - MLA_ROPE
    zl, zh = jnp.zeros((n, MLA_NOPE), F32), jnp.zeros((n, pad_hi), F32)
    cosq = jnp.concatenate([jnp.full((n, MLA_NOPE), qs, F32), cos * qs, zh], axis=1)
    sinq = jnp.concatenate([zl, sin * qs, zh], axis=1)
    cosk = jnp.concatenate([zl, cos, zh], axis=1)
    sink = jnp.concatenate([zl, sin, zh], axis=1)
    return cosq, sinq, cosk, sink


def _tile(n, want):
    return min(n, want)


def kernel(x, c, ctx, c_ctx, ada_w, ada_b, norm1_g, norm2_g, w_in, fourier_w, conv_w, pool_w, pool_scale,
           q_norm_g, w_uq, kv_norm_g, w_ukv, w_out, mlp_w1, mlp_w2, final_norm_g):
    b, n, d = x.shape
    nc = ctx.shape[1]
    depth = ada_w.shape[0]

    rows = -(-(b + 1) // SUBLANES) * SUBLANES
    cc = jnp.concatenate([c, c_ctx[None, :], jnp.zeros((rows - b - 1, d), F32)], axis=0)
    mod = _ada(cc, ada_w, ada_b)
    cw = _fold_fourier(fourier_w)
    half_dft = n >= 2 * TOKEN_TILE
    g_x = _dft_matrix(n, n // 2 if half_dft else n)
    g_c = _dft_matrix(nc, nc)
    sign = 1.0 - 2.0 * (jnp.arange(n) % 2).astype(F32)
    alt_row = jnp.zeros((SUBLANES, n), F32).at[0].set(sign / math.sqrt(n)).astype(BF16)
    tabs_x = _rope_tables(n)
    tabs_c = _place_tables(jnp.ones((nc, MLA_ROPE), F32), jnp.zeros((nc, MLA_ROPE), F32))
    fg = final_norm_g.reshape(1, d)

    def stream_mod(l, ctx_stream):
        m = mod[l, b:b + 1] if ctx_stream else mod[l, :b]
        m = jnp.broadcast_to(m, (b, 6 * d)).reshape(b, 1, 6, d)
        return [m[:, :, j, :] for j in range(6)]

    for l in range(depth):
        last = l == depth - 1
        win, wq, wk, wvt, poolw = _layout_weights(w_in[l], w_uq[l], w_ukv[l], pool_w[l])
        n1g = norm1_g[l].reshape(1, d)
        n2g = norm2_g[l].reshape(1, d)
        qg = q_norm_g[l].reshape(1, -1)
        kvg = kv_norm_g[l].reshape(1, -1)
        pscale = pool_scale[l].reshape(1, -1)
        wout = w_out[l].astype(BF16)
        w1 = mlp_w1[l].astype(BF16)
        w2 = mlp_w2[l].astype(BF16)

        def project(tokens, ctx_stream, tabs):
            sh1, sc1 = stream_mod(l, ctx_stream)[:2]
            t = _tile(tokens.shape[1], PROJ_TILE)
            return _proj(tokens, sc1, sh1, n1g, win, cw[l], qg, wq, kvg, wk, wvt, *tabs, t)

        def finish(tokens, ctx_stream, cp, ab, g, q, kv_sets, final):
            nt = tokens.shape[1]
            t = _tile(nt, TOKEN_TILE)
            _, _, g1, sh2, sc2, g2 = stream_mod(l, ctx_stream)
            cols = b * GROUP_W
            if g.shape[0] < nt:
                direct, mirrored = _dft_half(g, ab, _tile(nt // 2, 1024), _tile(cols, 1024), _tile(nt, 2048))
                middle = _matmul(alt_row, ab.reshape(2 * nt, cols), SUBLANES, _tile(cols, 2048), _tile(nt, 1024))
                f = (direct, mirrored, middle)
            else:
                f = _matmul(g, ab.reshape(2 * nt, cols), _tile(nt, 1024), _tile(cols, 2048), _tile(2 * nt, 1024))
            if t == 2 * ATTN_TILE:
                ats = _attn_pipelined(q, kv_sets, ATTN_TILE)
            else:
                ats = [[_attn(q, kv_sets, t).reshape(b * nt // t, t, GROUP_W)]]
            return _mixmlp(tokens, cp, f, ats, g1, sc2, sh2, g2, conv_w[l], poolw, pscale, wout, n2g, w1, w2,
                           fg, t, final)

        ab_c, cp_c, q_c, k_c, vt_c = project(ctx, True, tabs_c)
        ab_x, cp_x, q_x, k_x, vt_x = project(x, False, tabs_x)
        x = finish(x, False, cp_x, ab_x, g_x, q_x, [(k_x, vt_x), (k_c, vt_c)], last)
        if not last:
            ctx = finish(ctx, True, cp_c, ab_c, g_c, q_c, [(k_c, vt_c)], False)
    return x
```

```python
import functools
import math

import jax
import jax.numpy as jnp
from jax import lax
from jax.experimental import pallas as pl
from jax.experimental.pallas import tpu as pltpu

D_MODEL = 1024
DEPTH = 2
GRID_W = 64
GROUP_W = D_MODEL // 4
N_SUB = 4
SUB_W = GROUP_W // N_SUB
POOL_WINDOWS = (2, 4, 8, 16)
MLA_HEADS = 4
MLA_NOPE = GROUP_W // MLA_HEADS
MLA_ROPE = MLA_NOPE // 2
MLA_V = GROUP_W // MLA_HEADS
MLA_Q_RANK = GROUP_W
MLA_KV_RANK = GROUP_W // 2
MLA_SCALE = 1.0 / math.sqrt(MLA_NOPE + MLA_ROPE)
ROPE_BASE = 10000.0
D_FF = 4 * D_MODEL
EPS = 1e-6

OFF_CONV = GROUP_W
OFF_POOL = OFF_CONV + 3 * GROUP_W
OFF_MLA_Q = OFF_POOL + GROUP_W
OFF_MLA_KV = OFF_MLA_Q + MLA_Q_RANK
OFF_MLA_KR = OFF_MLA_KV + MLA_KV_RANK
IN_COLS = OFF_MLA_KR + MLA_ROPE

LANES = 128
SUBLANES = 8
HEAD_PAD = LANES
EXT_COLS = OFF_MLA_KR + LANES
DFT_ROWS = 64
HALO = SUBLANES
MLP_CHUNKS = 4
MIX_SUB_ROWS = 512
PROJ_SUB_ROWS = 512
VMEM_LIMIT = 52 * 1024 * 1024
ATTN_VMEM_LIMIT = 58 * 1024 * 1024
ATTN_SEG = 2048
ATTN_TILE = 256
TOKEN_TILE = 512
PROJ_TILE = 1024
LOG2E = math.log2(math.e)

F32 = jnp.float32
BF16 = jnp.bfloat16


_NT = (((1,), (1,)), ((), ()))


def _dot(a, b):
    return jnp.dot(a, b, preferred_element_type=F32)


def _rms(x):
    return x * lax.rsqrt(jnp.mean(x * x, axis=-1, keepdims=True) + EPS)


def _params(*sem):
    return pltpu.CompilerParams(dimension_semantics=sem, vmem_limit_bytes=VMEM_LIMIT)


def _const_spec(shape):
    nd = len(shape)
    return pl.BlockSpec(shape, lambda *_: (0,) * nd, pipeline_mode=pl.Buffered(1))


def _ada_kernel(cc_ref, w_ref, b_ref, o_ref):
    cc = cc_ref[...]
    s = cc * (1.0 / (1.0 + jnp.exp(-cc)))
    o_ref[0] = _dot(s.astype(BF16), w_ref[0].astype(BF16)) + b_ref[0]


def _ada(cc, ada_w, ada_b):
    depth, d, cols = ada_w.shape
    rows = cc.shape[0]
    tn = 1536
    return pl.pallas_call(
        _ada_kernel,
        out_shape=jax.ShapeDtypeStruct((depth, rows, cols), F32),
        grid=(depth, cols // tn),
        in_specs=[pl.BlockSpec((rows, d), lambda l, j: (0, 0)),
                  pl.BlockSpec((1, d, tn), lambda l, j: (l, 0, j)),
                  pl.BlockSpec((1, 1, tn), lambda l, j: (l, 0, j))],
        out_specs=pl.BlockSpec((1, rows, tn), lambda l, j: (l, 0, j)),
        compiler_params=_params("arbitrary", "arbitrary"),
        name="ada",
    )(cc, ada_w, ada_b.reshape(depth, 1, cols))


def _fold_kernel(dft_ref, w_ref, o_ref):
    w = w_ref[0]
    c = jnp.dot(dft_ref[:, :GROUP_W], w, preferred_element_type=F32, precision=lax.Precision.HIGHEST)
    s = jnp.dot(dft_ref[:, GROUP_W:], w, preferred_element_type=F32, precision=lax.Precision.HIGHEST)
    o_ref[0, :, :GROUP_W] = c.astype(BF16)
    o_ref[0, :, GROUP_W:] = s.astype(BF16)


def _channel_dft():
    i = jnp.arange(GROUP_W, dtype=jnp.int32)
    same = (i[:, None] // SUB_W) == (i[None, :] // SUB_W)
    ang = ((i[:, None] % SUB_W) * (i[None, :] % SUB_W) % SUB_W).astype(F32) * (2.0 * math.pi / SUB_W)
    scale = 1.0 / math.sqrt(SUB_W)
    c = jnp.where(same, jnp.cos(ang), 0.0) * scale
    s = jnp.where(same, jnp.sin(ang), 0.0) * scale
    return jnp.concatenate([c, s], axis=1)


def _fold_fourier(fourier_w):
    depth = fourier_w.shape[0]
    return pl.pallas_call(
        _fold_kernel,
        out_shape=jax.ShapeDtypeStruct((depth, GROUP_W, 2 * GROUP_W), BF16),
        grid=(depth,),
        in_specs=[pl.BlockSpec((GROUP_W, 2 * GROUP_W), lambda l: (0, 0)),
                  pl.BlockSpec((1, GROUP_W, GROUP_W), lambda l: (l, 0, 0))],
        out_specs=pl.BlockSpec((1, GROUP_W, 2 * GROUP_W), lambda l: (l, 0, 0)),
        compiler_params=_params("arbitrary"),
        name="fold_fourier",
    )(_channel_dft(), fourier_w)


def _dftgen_kernel(n, c1_ref, s1_ref, cb_ref, sb_ref, o_ref):
    c1 = c1_ref[0]
    s1 = s1_ref[0]
    cb = cb_ref[...]
    sb = sb_ref[...]
    scale = 1.0 / math.sqrt(n)
    o_ref[:, :n] = ((c1 * cb - s1 * sb) * scale).astype(BF16)
    o_ref[:, n:] = ((s1 * cb + c1 * sb) * (-scale)).astype(BF16)


def _dft_matrix(n, rows):
    nb = rows // DFT_ROWS
    k = jnp.arange(n, dtype=jnp.int32)
    j1 = jnp.arange(nb, dtype=jnp.int32) * DFT_ROWS
    j0 = jnp.arange(DFT_ROWS, dtype=jnp.int32)
    w = 2.0 * math.pi / n
    a1 = ((j1[:, None] * k[None, :]) % n).astype(F32) * w
    a0 = ((j0[:, None] * k[None, :]) % n).astype(F32) * w
    row_spec = pl.BlockSpec((1, 1, n), lambda i: (i, 0, 0))
    tab_spec = pl.BlockSpec((DFT_ROWS, n), lambda i: (0, 0))
    return pl.pallas_call(
        functools.partial(_dftgen_kernel, n),
        out_shape=jax.ShapeDtypeStruct((rows, 2 * n), BF16),
        grid=(nb,),
        in_specs=[row_spec, row_spec, tab_spec, tab_spec],
        out_specs=pl.BlockSpec((DFT_ROWS, 2 * n), lambda i: (i, 0)),
        compiler_params=_params("arbitrary"),
        name="dft_matrix",
    )(jnp.cos(a1).reshape(nb, 1, n), jnp.sin(a1).reshape(nb, 1, n), jnp.cos(a0), jnp.sin(a0))


def _matmul_kernel(a_ref, b_ref, o_ref, acc_ref):
    k = pl.program_id(2)

    @pl.when(k == 0)
    def _():
        acc_ref[...] = jnp.zeros_like(acc_ref)

    acc_ref[...] += _dot(a_ref[...], b_ref[...])

    @pl.when(k == pl.num_programs(2) - 1)
    def _():
        o_ref[...] = acc_ref[...].astype(o_ref.dtype)


def _matmul(a, b, tm, tn, tk):
    m, kk = a.shape
    n = b.shape[1]
    return pl.pallas_call(
        _matmul_kernel,
        out_shape=jax.ShapeDtypeStruct((m, n), BF16),
        grid=(m // tm, n // tn, kk // tk),
        in_specs=[pl.BlockSpec((tm, tk), lambda i, j, k: (i, k)),
                  pl.BlockSpec((tk, tn), lambda i, j, k: (k, j))],
        out_specs=pl.BlockSpec((tm, tn), lambda i, j, k: (i, j)),
        scratch_shapes=[pltpu.VMEM((tm, tn), F32)],
        compiler_params=_params("parallel", "parallel", "arbitrary"),
        name="dft_matmul",
    )(a, b)


def _dft_half_kernel(c_ref, s_ref, a_ref, b_ref, direct_ref, mirrored_ref, accp_ref, accq_ref):
    k = pl.program_id(2)

    @pl.when(k == 0)
    def _():
        accp_ref[...] = jnp.zeros_like(accp_ref)
        accq_ref[...] = jnp.zeros_like(accq_ref)

    accp_ref[...] += _dot(c_ref[...], a_ref[0])
    accq_ref[...] += _dot(s_ref[...], b_ref[0])

    @pl.when(k == pl.num_programs(2) - 1)
    def _():
        direct_ref[...] = (accp_ref[...] + accq_ref[...]).astype(direct_ref.dtype)
        mirrored_ref[...] = (accp_ref[...] - accq_ref[...]).astype(mirrored_ref.dtype)


def _dft_half(g_top, ab, tm, tn, tk):
    h, n2 = g_top.shape
    n = n2 // 2
    cols = ab.shape[2]
    out = jax.ShapeDtypeStruct((h, cols), BF16)
    kb = n // tk
    return pl.pallas_call(
        _dft_half_kernel,
        out_shape=(out, out),
        grid=(h // tm, cols // tn, kb),
        in_specs=[pl.BlockSpec((tm, tk), lambda i, j, k: (i, k)),
                  pl.BlockSpec((tm, tk), lambda i, j, k: (i, k + kb)),
                  pl.BlockSpec((1, tk, tn), lambda i, j, k: (0, k, j)),
                  pl.BlockSpec((1, tk, tn), lambda i, j, k: (1, k, j))],
        out_specs=(pl.BlockSpec((tm, tn), lambda i, j, k: (i, j)),
                   pl.BlockSpec((tm, tn), lambda i, j, k: (i, j))),
        scratch_shapes=[pltpu.VMEM((tm, tn), F32), pltpu.VMEM((tm, tn), F32)],
        compiler_params=_params("parallel", "parallel", "arbitrary"),
        name="dft_half",
    )(g_top, g_top, ab, ab)


def _rope_rotate(v):
    quarter = MLA_ROPE // 4
    up = pltpu.roll(v, LANES - quarter, axis=1)
    down = pltpu.roll(v, quarter, axis=1)
    lane = lax.broadcasted_iota(jnp.int32, v.shape, 1)
    first = ((lane - MLA_NOPE) // quarter) % 2 == 0
    return jnp.where(first, -up, down)


def _proj_kernel(x_ref, sc_ref, sh_ref, g_ref, win_ref, cw_ref, qg_ref, wq_ref, kvg_ref, wk_ref, wvt_ref,
                 cosq_ref, sinq_ref, cosk_ref, sink_ref, ab_ref, cp_ref, q_ref, k_ref, vt_ref):
    gain = g_ref[...] * (1.0 + sc_ref[0])
    shift = sh_ref[0]
    t = x_ref.shape[1]
    sub = min(t, PROJ_SUB_ROWS)
    for r0 in range(0, t, sub):
        rows = slice(r0, r0 + sub)
        hb = (_rms(x_ref[0, rows]) * gain + shift).astype(BF16)

        cq = _dot(hb, win_ref[:, OFF_MLA_Q:OFF_MLA_KV])
        ckv = _dot(hb, win_ref[:, OFF_MLA_KV:OFF_MLA_KR])
        u = _dot(hb, win_ref[:, :OFF_CONV])
        kp1 = _dot(hb, win_ref[:, OFF_MLA_KR:EXT_COLS])
        cp_ref[0, rows] = _dot(hb, win_ref[:, OFF_CONV:OFF_MLA_Q])

        cqn = (_rms(cq) * qg_ref[...]).astype(BF16)
        qq = _dot(cqn, wq_ref[...])
        cosq = cosq_ref[rows]
        sinq = sinq_ref[rows]
        for hd in range(MLA_HEADS):
            lo = hd * HEAD_PAD
            qh = qq[:, lo:lo + HEAD_PAD]
            q_ref[0, rows, lo:lo + HEAD_PAD] = (qh * cosq + _rope_rotate(qh) * sinq).astype(BF16)

        ab = _dot(u.astype(BF16), cw_ref[...])
        ab_ref[0, rows] = ab[:, :GROUP_W].astype(BF16)
        ab_ref[1, rows] = ab[:, GROUP_W:].astype(BF16)

        ckvn = (_rms(ckv) * kvg_ref[...]).astype(BF16)
        kn = _dot(ckvn, wk_ref[...])
        kp = kp1 * cosk_ref[rows] + _rope_rotate(kp1) * sink_ref[rows]
        for hd in range(MLA_HEADS):
            lo = hd * HEAD_PAD
            k_ref[0, rows, lo:lo + HEAD_PAD] = (kn[:, lo:lo + HEAD_PAD] + kp).astype(BF16)
        vt_ref[0, :, rows] = lax.dot_general(wvt_ref[...], ckvn, _NT, preferred_element_type=F32).astype(BF16)


def _proj(x, scale, shift, g, win, cw, qg, wq, kvg, wk, wvt, cosq, sinq, cosk, sink, t):
    b, n, d = x.shape
    row = lambda bi, i: (bi, 0, 0)
    tile = lambda bi, i: (bi, i, 0)
    tab = pl.BlockSpec((t, LANES), lambda bi, i: (i, 0))
    qk_w = MLA_HEADS * HEAD_PAD
    return pl.pallas_call(
        _proj_kernel,
        out_shape=(jax.ShapeDtypeStruct((2, n, b * GROUP_W), BF16),
                   jax.ShapeDtypeStruct((b, n, 4 * GROUP_W), F32),
                   jax.ShapeDtypeStruct((b, n, qk_w), BF16),
                   jax.ShapeDtypeStruct((b, n, qk_w), BF16),
                   jax.ShapeDtypeStruct((b, GROUP_W, n), BF16)),
        grid=(b, n // t),
        in_specs=[pl.BlockSpec((1, t, d), tile),
                  pl.BlockSpec((1, 1, d), row),
                  pl.BlockSpec((1, 1, d), row),
                  _const_spec(g.shape), _const_spec(win.shape), _const_spec(cw.shape),
                  _const_spec(qg.shape), _const_spec(wq.shape), _const_spec(kvg.shape),
                  _const_spec(wk.shape), _const_spec(wvt.shape), tab, tab, tab, tab],
        out_specs=(pl.BlockSpec((2, t, GROUP_W), lambda bi, i: (0, i, bi)),
                   pl.BlockSpec((1, t, 4 * GROUP_W), tile),
                   pl.BlockSpec((1, t, qk_w), tile),
                   pl.BlockSpec((1, t, qk_w), tile),
                   pl.BlockSpec((1, GROUP_W, t), lambda bi, i: (bi, 0, i))),
        compiler_params=_params("parallel", "parallel"),
        name="proj",
    )(x, scale, shift, g, win, cw, qg, wq, kvg, wk, wvt, cosq, sinq, cosk, sink)


def _key_segments(k_refs):
    segs = []
    off = 0
    for j, k_ref in enumerate(k_refs):
        nk = k_ref.shape[1]
        size = min(nk, ATTN_SEG)
        for start in range(0, nk, size):
            segs.append((j, start, size, off + start))
        off += nk
    return segs


def _seg_scores(q_ref, k_refs, hd, seg, s_ref):
    j, start, size, off = seg
    lo = hd * HEAD_PAD
    s = lax.dot_general(k_refs[j][0, start:start + size, lo:lo + HEAD_PAD], q_ref[0, :, lo:lo + HEAD_PAD], _NT,
                        preferred_element_type=F32)
    s_ref[hd, off:off + size] = s
    return jnp.max(s, axis=0, keepdims=True)


def _seg_softmax_pv(s_ref, m, vt_refs, hd, seg):
    j, start, size, off = seg
    p = jnp.exp2(s_ref[hd, off:off + size] - m)
    return (jnp.sum(p, axis=0, keepdims=True),
            _dot(vt_refs[j][0, hd * MLA_V:(hd + 1) * MLA_V, start:start + size], p.astype(BF16)))


def _head_pipeline(q_ref, k_refs, s_new, m_new, s_old, m_old, vt_refs, hd):
    m_prev = None if s_old is None else m_old[hd]
    m = l = o = None
    for seg in _key_segments(k_refs):
        if s_new is not None:
            mj = _seg_scores(q_ref, k_refs, hd, seg, s_new)
            m = mj if m is None else jnp.maximum(m, mj)
        if s_old is not None:
            lj, oj = _seg_softmax_pv(s_old, m_prev, vt_refs, hd, seg)
            l = lj if l is None else l + lj
            o = oj if o is None else o + oj
    if s_new is not None:
        m_new[hd] = m
    return None if s_old is None else o / l


def _store_heads(outs, o_ref):
    o_ref[0] = jnp.concatenate(outs, axis=0).T.astype(o_ref.dtype)


def _tile_scratch(n_keys, tq):
    return [pltpu.VMEM((MLA_HEADS, n_keys, tq), F32), pltpu.VMEM((MLA_HEADS, 1, tq), F32)]


def _attn_kernel(n_sets, q_ref, *refs):
    o_ref, s_ref, m_ref = refs[2 * n_sets:]
    k_refs, vt_refs = refs[:n_sets], refs[n_sets:2 * n_sets]
    for hd in range(MLA_HEADS):
        _head_pipeline(q_ref, k_refs, s_ref, m_ref, None, None, None, hd)
    _store_heads([_head_pipeline(None, k_refs, None, None, s_ref, m_ref, vt_refs, hd) for hd in range(MLA_HEADS)],
                 o_ref)


def _attn(q, kv_sets, tq):
    b, n, qk_w = q.shape
    whole = lambda bi, i: (bi, 0, 0)
    ks = [k for k, _ in kv_sets]
    vts = [vt for _, vt in kv_sets]
    return pl.pallas_call(
        functools.partial(_attn_kernel, len(kv_sets)),
        out_shape=jax.ShapeDtypeStruct((b, n, GROUP_W), BF16),
        grid=(b, n // tq),
        in_specs=([pl.BlockSpec((1, tq, qk_w), lambda bi, i: (bi, i, 0))]
                  + [pl.BlockSpec((1,) + k.shape[1:], whole) for k in ks]
                  + [pl.BlockSpec((1,) + vt.shape[1:], whole) for vt in vts]),
        out_specs=pl.BlockSpec((1, tq, GROUP_W), lambda bi, i: (bi, i, 0)),
        scratch_shapes=_tile_scratch(sum(k.shape[1] for k in ks), tq),
        compiler_params=_params("parallel", "arbitrary"),
        name="attn",
    )(q, *ks, *vts)


def _attn_pipe_kernel(n_sets, qe_ref, qo_ref, *refs):
    k_refs = refs[:n_sets]
    vt_prev = refs[n_sets:2 * n_sets]
    vt_cur = refs[2 * n_sets:3 * n_sets]
    oe_ref, oo_ref, s0, m0, s1, m1 = refs[3 * n_sets:]

    @pl.when(pl.program_id(0) == 0)
    def _():
        s1[...] = jnp.zeros_like(s1)
        m1[...] = jnp.zeros_like(m1)

    _store_heads([_head_pipeline(qe_ref, k_refs, s0, m0, s1, m1, vt_prev, hd) for hd in range(MLA_HEADS)], oo_ref)
    _store_heads([_head_pipeline(qo_ref, k_refs, s1, m1, s0, m0, vt_cur, hd) for hd in range(MLA_HEADS)], oe_ref)


def _attn_pipelined(q, kv_sets, tq):
    b, n, qk_w = q.shape
    nq = n // tq
    pairs = b * nq // 2
    ks = [k for k, _ in kv_sets]
    vts = [vt for _, vt in kv_sets]
    even = lambda j: jnp.minimum(2 * j, 2 * pairs - 2)
    b_cur = lambda j: even(j) // nq
    b_prev = lambda j: jnp.maximum(2 * j - 1, 0) // nq
    cur = lambda j: (jnp.minimum(j, pairs - 1), 0, 0)
    prev = lambda j: (jnp.maximum(j - 1, 0), 0, 0)
    once = pl.Buffered(1)
    out = jax.ShapeDtypeStruct((pairs, tq, GROUP_W), BF16)
    out_block = (1, tq, GROUP_W)
    n_keys = sum(k.shape[1] for k in ks)
    even_tiles, odd_tiles = pl.pallas_call(
        functools.partial(_attn_pipe_kernel, len(kv_sets)),
        out_shape=(out, out),
        grid=(pairs + 1,),
        in_specs=([pl.BlockSpec((1, tq, qk_w), lambda j: (b_cur(j), even(j) % nq, 0)),
                   pl.BlockSpec((1, tq, qk_w), lambda j: (b_cur(j), even(j) % nq + 1, 0))]
                  + [pl.BlockSpec((1,) + k.shape[1:], lambda j: (b_cur(j), 0, 0)) for k in ks]
                  + [pl.BlockSpec((1,) + vt.shape[1:], lambda j: (b_prev(j), 0, 0), pipeline_mode=once)
                     for vt in vts]
                  + [pl.BlockSpec((1,) + vt.shape[1:], lambda j: (b_cur(j), 0, 0)) for vt in vts]),
        out_specs=(pl.BlockSpec(out_block, cur), pl.BlockSpec(out_block, prev)),
        scratch_shapes=_tile_scratch(n_keys, tq) * 2,
        compiler_params=pltpu.CompilerParams(dimension_semantics=("arbitrary",),
                                             vmem_limit_bytes=ATTN_VMEM_LIMIT),
        name="attn_pipe",
    )(q, q, *ks, *vts, *vts)
    return [[even_tiles], [odd_tiles]]


def _fourier_tile(i, n_tiles, f_refs):
    if len(f_refs) == 1:
        return f_refs[0][...]
    direct_ref, mirrored_ref, mrow_ref, middle_ref, flip_ref = f_refs
    t = direct_ref.shape[0]
    flipped = _dot(flip_ref[...], mirrored_ref[...]).astype(BF16)
    row0 = jnp.where(2 * i == n_tiles, middle_ref[0:1, :], mrow_ref[0:1, :])
    is_row0 = lax.broadcasted_iota(jnp.int32, (t, GROUP_W), 0) == 0
    mirrored = jnp.where(is_row0, row0, flipped)
    return jnp.where(2 * i >= n_tiles, mirrored, direct_ref[...])


def _mixmlp_kernel(n_tokens, final, n_f, x_ref, cp_ref, prev_ref, next_ref, g1_ref, sc2_ref,
                   sh2_ref, g2_ref, convw_ref, poolw_ref, pscale_ref, wout_ref, n2g_ref, w1_ref, w2_ref,
                   fg_ref, *tail):
    f_refs, at_refs, o_ref = tail[:n_f], tail[n_f:-1], tail[-1]
    i = pl.program_id(1)
    t = x_ref.shape[1]
    first = i == 0
    last = i == pl.num_programs(1) - 1
    prev = jnp.where(first, 0.0, prev_ref[0])
    nxt = jnp.where(last, 0.0, next_ref[0])
    cpe = jnp.concatenate([prev, cp_ref[0], nxt], axis=0)
    sub = min(t, MIX_SUB_ROWS)
    ext = sub + 2 * HALO
    inner = slice(HALO, HALO + sub)
    lane = lax.broadcasted_iota(jnp.int32, (sub, LANES), 1)
    lo_half = lane < SUB_W
    per_row = GROUP_W // at_refs[0].shape[2]
    attn = jnp.concatenate([jnp.concatenate([r[0] for r in at_refs[k:k + per_row]], axis=-1)
                            for k in range(0, len(at_refs), per_row)], axis=0)
    fourier = _fourier_tile(i, pl.num_programs(1), f_refs)
    gain2 = n2g_ref[...] * (1.0 + sc2_ref[0])
    chunk = D_FF // MLP_CHUNKS

    def shifted(a, k):
        return pltpu.roll(a, k % ext, axis=0)

    def mixers(r0, st):
        ce = cpe[r0:r0 + ext]
        z = ce[:, GROUP_W:2 * GROUP_W] * ce[:, 2 * GROUP_W:3 * GROUP_W]
        y = shifted(z, 1) * convw_ref[0:1, :] + z * convw_ref[1:2, :] + shifted(z, -1) * convw_ref[2:3, :]
        st["conv"] = (ce[:, :GROUP_W] * y)[inner]
        u = ce[:, 3 * GROUP_W:]
        s2 = shifted(u, 1) + u
        s4 = shifted(s2, 1) + shifted(s2, -1)
        ub = u[:, LANES:]
        s4b = s4[:, LANES:]
        s8b = shifted(s4b, 2) + shifted(s4b, -2)
        s16b = shifted(s8b, 4) + shifted(s8b, -4)
        tok = lax.broadcasted_iota(jnp.int32, (sub, LANES), 0) + (i * t + r0)

        def mean(sum_lo, sum_hi, w_lo, w_hi):
            wh = jnp.where(lo_half, w_lo // 2, w_hi // 2)
            cnt = jnp.minimum(tok + wh - 1, n_tokens - 1) - jnp.maximum(tok - wh, 0) + 1
            return jnp.where(lo_half, sum_lo[inner], sum_hi[inner]) / cnt.astype(F32)

        pa = mean(s2[:, :LANES], s4[:, :LANES], POOL_WINDOWS[0], POOL_WINDOWS[1]) - u[inner, :LANES]
        pb = mean(s8b, s16b, POOL_WINDOWS[2], POOL_WINDOWS[3]) - ub[inner]
        pin = jnp.concatenate([pa, pb], axis=-1).astype(BF16)
        st["pool"] = _dot(pin, poolw_ref[...]) * pscale_ref[...]

    def out_proj(r0, st):
        rows = slice(r0, r0 + sub)
        mix = jnp.concatenate([fourier[rows], st["conv"].astype(BF16), st["pool"].astype(BF16), attn[rows]], axis=-1)
        st["x1"] = x_ref[0, rows] + g1_ref[0] * _dot(mix, wout_ref[...])

    def norm2(r0, st):
        st["h2"] = (_rms(st["x1"]) * gain2 + sh2_ref[0]).astype(BF16)
        st["acc"] = None

    def mlp_chunk(c, st):
        hid = jnp.maximum(_dot(st["h2"], w1_ref[:, c * chunk:(c + 1) * chunk]), 0.0)
        part = _dot((hid * hid).astype(BF16), w2_ref[c * chunk:(c + 1) * chunk, :])
        st["acc"] = part if st["acc"] is None else st["acc"] + part

    def finish(r0, st):
        x2 = st["x1"] + g2_ref[0] * st["acc"]
        if final:
            x2 = _rms(x2) * fg_ref[...]
        o_ref[0, r0:r0 + sub] = x2

    prologue = (mixers, out_proj, norm2)
    states = [dict() for _ in range(0, t, sub)]
    for step in prologue:
        step(0, states[0])
    for bi, st in enumerate(states):
        nxt_r0 = (bi + 1) * sub
        for c in range(MLP_CHUNKS):
            mlp_chunk(c, st)
            if bi + 1 < len(states) and c < len(prologue):
                prologue[c](nxt_r0, states[bi + 1])
        finish(bi * sub, st)


def _mixmlp(x, cp, f, ats, g1, sc2, sh2, g2, convw, poolw, pscale, wout, n2g, w1, w2, fg, t, final):
    b, n, d = x.shape
    if isinstance(f, tuple):
        direct, mirrored, middle = f
        nt, nh, hb8 = n // t, n // t // 2, t // SUBLANES
        r = jnp.arange(t, dtype=jnp.int32)
        flip = ((r[:, None] + r[None, :]) == t).astype(BF16)
        fs = [direct, mirrored, mirrored, middle, flip]
        f_specs = [pl.BlockSpec((t, GROUP_W), lambda bi, i: (jnp.minimum(i, nh - 1), bi)),
                   pl.BlockSpec((t, GROUP_W), lambda bi, i: (jnp.clip(nt - 1 - i, 0, nh - 1), bi)),
                   pl.BlockSpec((SUBLANES, GROUP_W), lambda bi, i: (jnp.clip((nt - i) * hb8, 0, nh * hb8 - 1), bi)),
                   pl.BlockSpec((SUBLANES, GROUP_W), lambda bi, i: (0, bi)),
                   _const_spec(flip.shape)]
    else:
        fs = [f]
        f_specs = [pl.BlockSpec((t, GROUP_W), lambda bi, i: (i, bi))]
    at_specs = [pl.BlockSpec((1,) + a.shape[1:], lambda bi, i: (bi * (n // t) + i, 0, 0)) for r in ats for a in r]
    ats = [a for r in ats for a in r]
    hb = t // HALO
    nhb = n // HALO
    row = lambda bi, i: (bi, 0, 0)
    tile = lambda bi, i: (bi, i, 0)
    row_spec = pl.BlockSpec((1, 1, d), row)
    return pl.pallas_call(
        functools.partial(_mixmlp_kernel, n, final, len(fs)),
        out_shape=jax.ShapeDtypeStruct((b, n, d), F32),
        grid=(b, n // t),
        in_specs=[pl.BlockSpec((1, t, d), tile),
                  pl.BlockSpec((1, t, 4 * GROUP_W), tile),
                  pl.BlockSpec((1, HALO, 4 * GROUP_W), lambda bi, i: (bi, jnp.maximum(i * hb - 1, 0), 0)),
                  pl.BlockSpec((1, HALO, 4 * GROUP_W), lambda bi, i: (bi, jnp.minimum((i + 1) * hb, nhb - 1), 0)),
                  row_spec, row_spec, row_spec, row_spec,
                  _const_spec(convw.shape), _const_spec(poolw.shape), _const_spec(pscale.shape),
                  _const_spec(wout.shape), _const_spec(n2g.shape), _const_spec(w1.shape),
                  _const_spec(w2.shape), _const_spec(fg.shape)] + f_specs + at_specs,
        out_specs=pl.BlockSpec((1, t, d), tile),
        compiler_params=_params("parallel", "arbitrary"),
        name="mixmlp",
    )(x, cp, cp, cp, g1, sc2, sh2, g2, convw, poolw, pscale, wout, n2g, w1, w2, fg, *fs, *ats)


def _layout_weights(w_in, w_uq, w_ukv, pool_w):
    d = w_in.shape[0]
    kr = w_in[:, OFF_MLA_KR:]
    z = lambda r, c: jnp.zeros((r, c), F32)
    pad_lo, pad_hi = MLA_NOPE, HEAD_PAD - MLA_NOPE - MLA_ROPE
    win = jnp.concatenate([w_in[:, :OFF_MLA_KR], z(d, pad_lo), kr, z(d, pad_hi)], axis=1)
    qh = w_uq.reshape(MLA_Q_RANK, MLA_HEADS, MLA_NOPE + MLA_ROPE)
    wq = jnp.pad(qh, ((0, 0), (0, 0), (0, pad_hi))).reshape(MLA_Q_RANK, -1)
    kvh =w_ukv.reshape(MLA_KV_RANK, MLA_HEADS, MLA_NOPE + MLA_V)
    k_plain = jnp.pad(kvh[..., :MLA_NOPE], ((0, 0), (0, 0), (0, HEAD_PAD - MLA_NOPE)))
    wk = k_plain.reshape(MLA_KV_RANK, -1)
    wvt = kvh[..., MLA_NOPE:].reshape(MLA_KV_RANK, -1).T
    poolw = jax.scipy.linalg.block_diag(*[pool_w[g] for g in range(N_SUB)])
    return win.astype(BF16), wq.astype(BF16), wk.astype(BF16), wvt.astype(BF16), poolw.astype(BF16)


def _rope_tables(n):
    rows = n // GRID_W
    row = jnp.repeat(jnp.arange(rows, dtype=F32), GRID_W)
    col = jnp.tile(jnp.arange(GRID_W, dtype=F32), rows)
    half = MLA_ROPE // 2
    inv = ROPE_BASE ** (-jnp.arange(0, half, 2, dtype=F32) / half)
    ang_r = row[:, None] * inv[None, :]
    ang_c = col[:, None] * inv[None, :]
    ang = jnp.concatenate([ang_r, ang_r, ang_c, ang_c], axis=-1)
    return _place_tables(jnp.cos(ang), jnp.sin(ang))


def _place_tables(cos, sin):
    n = cos.shape[0]
    qs = MLA_SCALE * LOG2E
    pad_hi = HEAD_PAD - MLA_NOPE - MLA_ROPE
    zl, zh = jnp.zeros((n, MLA_NOPE), F32), jnp.zeros((n, pad_hi), F32)
    cosq = jnp.concatenate([jnp.full((n, MLA_NOPE), qs, F32), cos * qs, zh], axis=1)
    sinq = jnp.concatenate([zl, sin * qs, zh], axis=1)
    cosk = jnp.concatenate([zl, cos, zh], axis=1)
    sink = jnp.concatenate([zl, sin, zh], axis=1)
    return cosq, sinq, cosk, sink


def _tile(n, want):
    return min(n, want)


def kernel(x, c, ctx, c_ctx, ada_w, ada_b, norm1_g, norm2_g, w_in, fourier_w, conv_w, pool_w, pool_scale,
           q_norm_g, w_uq, kv_norm_g, w_ukv, w_out, mlp_w1, mlp_w2, final_norm_g):
    b, n, d = x.shape
    nc = ctx.shape[1]
    depth = ada_w.shape[0]

    rows = -(-(b + 1) // SUBLANES) * SUBLANES
    cc = jnp.concatenate([c, c_ctx[None, :], jnp.zeros((rows - b - 1, d), F32)], axis=0)
    mod = _ada(cc, ada_w, ada_b)
    cw = _fold_fourier(fourier_w)
    half_dft = n >= 2 * TOKEN_TILE
    g_x = _dft_matrix(n, n // 2 if half_dft else n)
    g_c = _dft_matrix(nc, nc)
    sign = 1.0 - 2.0 * (jnp.arange(n) % 2).astype(F32)
    alt_row = jnp.zeros((SUBLANES, n), F32).at[0].set(sign / math.sqrt(n)).astype(BF16)
    tabs_x = _rope_tables(n)
    tabs_c = _place_tables(jnp.ones((nc, MLA_ROPE), F32), jnp.zeros((nc, MLA_ROPE), F32))
    fg = final_norm_g.reshape(1, d)

    def stream_mod(l, ctx_stream):
        m = mod[l, b:b + 1] if ctx_stream else mod[l, :b]
        m = jnp.broadcast_to(m, (b, 6 * d)).reshape(b, 1, 6, d)
        return [m[:, :, j, :] for j in range(6)]

    for l in range(depth):
        last = l == depth - 1
        win, wq, wk, wvt, poolw = _layout_weights(w_in[l], w_uq[l], w_ukv[l], pool_w[l])
        n1g = norm1_g[l].reshape(1, d)
        n2g = norm2_g[l].reshape(1, d)
        qg = q_norm_g[l].reshape(1, -1)
        kvg = kv_norm_g[l].reshape(1, -1)
        pscale = pool_scale[l].reshape(1, -1)
        wout = w_out[l].astype(BF16)
        w1 = mlp_w1[l].astype(BF16)
        w2 = mlp_w2[l].astype(BF16)

        def project(tokens, ctx_stream, tabs):
            sh1, sc1 = stream_mod(l, ctx_stream)[:2]
            t = _tile(tokens.shape[1], PROJ_TILE)
            return _proj(tokens, sc1, sh1, n1g, win, cw[l], qg, wq, kvg, wk, wvt, *tabs, t)

        def finish(tokens, ctx_stream, cp, ab, g, q, kv_sets, final):
            nt = tokens.shape[1]
            t = _tile(nt, TOKEN_TILE)
            _, _, g1, sh2, sc2, g2 = stream_mod(l, ctx_stream)
            cols = b * GROUP_W
            if g.shape[0] < nt:
                direct, mirrored = _dft_half(g, ab, _tile(nt // 2, 1024), _tile(cols, 1024), _tile(nt, 2048))
                middle = _matmul(alt_row, ab.reshape(2 * nt, cols), SUBLANES, _tile(cols, 2048), _tile(nt, 1024))
                f = (direct, mirrored, middle)
            else:
                f = _matmul(g, ab.reshape(2 * nt, cols), _tile(nt, 1024), _tile(cols, 2048), _tile(2 * nt, 1024))
            if t == 2 * ATTN_TILE:
                ats = _attn_pipelined(q, kv_sets, ATTN_TILE)
            else:
                ats = [[_attn(q, kv_sets, t).reshape(b * nt // t, t, GROUP_W)]]
            return _mixmlp(tokens, cp, f, ats, g1, sc2, sh2, g2, conv_w[l], poolw, pscale, wout, n2g, w1, w2,
                           fg, t, final)

        ab_c, cp_c, q_c, k_c, vt_c = project(ctx, True, tabs_c)
        ab_x, cp_x, q_x, k_x, vt_x = project(x, False, tabs_x)
        x = finish(x, False, cp_x, ab_x, g_x, q_x, [(k_x, vt_x), (k_c, vt_c)], last)
        if not last:
            ctx = finish(ctx, True, cp_c, ab_c, g_c, q_c, [(k_c, vt_c)], False)
    return x
```

```python
import functools
import math

import jax
import jax.numpy as jnp
from jax import lax
from jax.experimental import pallas as pl
from jax.experimental.pallas import tpu as pltpu

D_MODEL = 1024
DEPTH = 2
GRID_W = 64
GROUP_W = D_MODEL // 4
N_SUB = 4
SUB_W = GROUP_W // N_SUB
POOL_WINDOWS = (2, 4, 8, 16)
MLA_HEADS = 4
MLA_NOPE = GROUP_W // MLA_HEADS
MLA_ROPE = MLA_NOPE // 2
MLA_V = GROUP_W // MLA_HEADS
MLA_Q_RANK = GROUP_W
MLA_KV_RANK = GROUP_W // 2
MLA_SCALE = 1.0 / math.sqrt(MLA_NOPE + MLA_ROPE)
ROPE_BASE = 10000.0
D_FF = 4 * D_MODEL
EPS = 1e-6

OFF_CONV = GROUP_W
OFF_POOL = OFF_CONV + 3 * GROUP_W
OFF_MLA_Q = OFF_POOL + GROUP_W
OFF_MLA_KV = OFF_MLA_Q + MLA_Q_RANK
OFF_MLA_KR = OFF_MLA_KV + MLA_KV_RANK
IN_COLS = OFF_MLA_KR + MLA_ROPE

LANES = 128
SUBLANES = 8
HEAD_PAD = LANES
EXT_COLS = OFF_MLA_KR + LANES
DFT_ROWS = 64
HALO = SUBLANES
MLP_CHUNKS = 4
MIX_SUB_ROWS = 512
PROJ_SUB_ROWS = 512
VMEM_LIMIT = 52 * 1024 * 1024
ATTN_VMEM_LIMIT = 58 * 1024 * 1024
ATTN_SEG = 2048
ATTN_TILE = 256
TOKEN_TILE = 512
PROJ_TILE = 1024
LOG2E = math.log2(math.e)

F32 = jnp.float32
BF16 = jnp.bfloat16


_NT = (((1,), (1,)), ((), ()))


def _dot(a, b):
    return jnp.dot(a, b, preferred_element_type=F32)


def _rms(x):
    return x * lax.rsqrt(jnp.mean(x * x, axis=-1, keepdims=True) + EPS)


def _params(*sem):
    return pltpu.CompilerParams(dimension_semantics=sem, vmem_limit_bytes=VMEM_LIMIT)


def _const_spec(shape):
    nd = len(shape)
    return pl.BlockSpec(shape, lambda *_: (0,) * nd, pipeline_mode=pl.Buffered(1))


def _ada_kernel(cc_ref, w_ref, b_ref, o_ref):
    cc = cc_ref[...]
    s = cc * (1.0 / (1.0 + jnp.exp(-cc)))
    o_ref[0] = _dot(s.astype(BF16), w_ref[0].astype(BF16)) + b_ref[0]


def _ada(cc, ada_w, ada_b):
    depth, d, cols = ada_w.shape
    rows = cc.shape[0]
    tn = 1536
    return pl.pallas_call(
        _ada_kernel,
        out_shape=jax.ShapeDtypeStruct((depth, rows, cols), F32),
        grid=(depth, cols // tn),
        in_specs=[pl.BlockSpec((rows, d), lambda l, j: (0, 0)),
                  pl.BlockSpec((1, d, tn), lambda l, j: (l, 0, j)),
                  pl.BlockSpec((1, 1, tn), lambda l, j: (l, 0, j))],
        out_specs=pl.BlockSpec((1, rows, tn), lambda l, j: (l, 0, j)),
        compiler_params=_params("arbitrary", "arbitrary"),
        name="ada",
    )(cc, ada_w, ada_b.reshape(depth, 1, cols))


def _fold_kernel(dft_ref, w_ref, o_ref):
    w = w_ref[0]
    c = jnp.dot(dft_ref[:, :GROUP_W], w, preferred_element_type=F32, precision=lax.Precision.HIGHEST)
    s = jnp.dot(dft_ref[:, GROUP_W:], w, preferred_element_type=F32, precision=lax.Precision.HIGHEST)
    o_ref[0, :, :GROUP_W] = c.astype(BF16)
    o_ref[0, :, GROUP_W:] = s.astype(BF16)


def _channel_dft():
    i = jnp.arange(GROUP_W, dtype=jnp.int32)
    same = (i[:, None] // SUB_W) == (i[None, :] // SUB_W)
    ang = ((i[:, None] % SUB_W) * (i[None, :] % SUB_W) % SUB_W).astype(F32) * (2.0 * math.pi / SUB_W)
    scale = 1.0 / math.sqrt(SUB_W)
    c = jnp.where(same, jnp.cos(ang), 0.0) * scale
    s = jnp.where(same, jnp.sin(ang), 0.0) * scale
    return jnp.concatenate([c, s], axis=1)


def _fold_fourier(fourier_w):
    depth = fourier_w.shape[0]
    return pl.pallas_call(
        _fold_kernel,
        out_shape=jax.ShapeDtypeStruct((depth, GROUP_W, 2 * GROUP_W), BF16),
        grid=(depth,),
        in_specs=[pl.BlockSpec((GROUP_W, 2 * GROUP_W), lambda l: (0, 0)),
                  pl.BlockSpec((1, GROUP_W, GROUP_W), lambda l: (l, 0, 0))],
        out_specs=pl.BlockSpec((1, GROUP_W, 2 * GROUP_W), lambda l: (l, 0, 0)),
        compiler_params=_params("arbitrary"),
        name="fold_fourier",
    )(_channel_dft(), fourier_w)


def _dftgen_kernel(n, c1_ref, s1_ref, cb_ref, sb_ref, o_ref):
    c1 = c1_ref[0]
    s1 = s1_ref[0]
    cb = cb_ref[...]
    sb = sb_ref[...]
    scale = 1.0 / math.sqrt(n)
    o_ref[:, :n] = ((c1 * cb - s1 * sb) * scale).astype(BF16)
    o_ref[:, n:] = ((s1 * cb + c1 * sb) * (-scale)).astype(BF16)


def _dft_matrix(n, rows):
    nb = rows // DFT_ROWS
    k = jnp.arange(n, dtype=jnp.int32)
    j1 = jnp.arange(nb, dtype=jnp.int32) * DFT_ROWS
    j0 = jnp.arange(DFT_ROWS, dtype=jnp.int32)
    w = 2.0 * math.pi / n
    a1 = ((j1[:, None] * k[None, :]) % n).astype(F32) * w
    a0 = ((j0[:, None] * k[None, :]) % n).astype(F32) * w
    row_spec = pl.BlockSpec((1, 1, n), lambda i: (i, 0, 0))
    tab_spec = pl.BlockSpec((DFT_ROWS, n), lambda i: (0, 0))
    return pl.pallas_call(
        functools.partial(_dftgen_kernel, n),
        out_shape=jax.ShapeDtypeStruct((rows, 2 * n), BF16),
        grid=(nb,),
        in_specs=[row_spec, row_spec, tab_spec, tab_spec],
        out_specs=pl.BlockSpec((DFT_ROWS, 2 * n), lambda i: (i, 0)),
        compiler_params=_params("arbitrary"),
        name="dft_matrix",
    )(jnp.cos(a1).reshape(nb, 1, n), jnp.sin(a1).reshape(nb, 1, n), jnp.cos(a0), jnp.sin(a0))


def _matmul_kernel(a_ref, b_ref, o_ref, acc_ref):
    k = pl.program_id(2)

    @pl.when(k == 0)
    def _():
        acc_ref[...] = jnp.zeros_like(acc_ref)

    acc_ref[...] += _dot(a_ref[...], b_ref[...])

    @pl.when(k == pl.num_programs(2) - 1)
    def _():
        o_ref[...] = acc_ref[...].astype(o_ref.dtype)


def _matmul(a, b, tm, tn, tk):
    m, kk = a.shape
    n = b.shape[1]
    return pl.pallas_call(
        _matmul_kernel,
        out_shape=jax.ShapeDtypeStruct((m, n), BF16),
        grid=(m // tm, n // tn, kk // tk),
        in_specs=[pl.BlockSpec((tm, tk), lambda i, j, k: (i, k)),
                  pl.BlockSpec((tk, tn), lambda i, j, k: (k, j))],
        out_specs=pl.BlockSpec((tm, tn), lambda i, j, k: (i, j)),
        scratch_shapes=[pltpu.VMEM((tm, tn), F32)],
        compiler_params=_params("parallel", "parallel", "arbitrary"),
        name="dft_matmul",
    )(a, b)


def _dft_half_kernel(c_ref, s_ref, a_ref, b_ref, direct_ref, mirrored_ref, accp_ref, accq_ref):
    k = pl.program_id(2)

    @pl.when(k == 0)
    def _():
        accp_ref[...] = jnp.zeros_like(accp_ref)
        accq_ref[...] = jnp.zeros_like(accq_ref)

    accp_ref[...] += _dot(c_ref[...], a_ref[0])
    accq_ref[...] += _dot(s_ref[...], b_ref[0])

    @pl.when(k == pl.num_programs(2) - 1)
    def _():
        direct_ref[...] = (accp_ref[...] + accq_ref[...]).astype(direct_ref.dtype)
        mirrored_ref[...] = (accp_ref[...] - accq_ref[...]).astype(mirrored_ref.dtype)


def _dft_half(g_top, ab, tm, tn, tk):
    h, n2 = g_top.shape
    n = n2 // 2
    cols = ab.shape[2]
    out = jax.ShapeDtypeStruct((h, cols), BF16)
    kb = n // tk
    return pl.pallas_call(
        _dft_half_kernel,
        out_shape=(out, out),
        grid=(h // tm, cols // tn, kb),
        in_specs=[pl.BlockSpec((tm, tk), lambda i, j, k: (i, k)),
                  pl.BlockSpec((tm, tk), lambda i, j, k: (i, k + kb)),
                  pl.BlockSpec((1, tk, tn), lambda i, j, k: (0, k, j)),
                  pl.BlockSpec((1, tk, tn), lambda i, j, k: (1, k, j))],
        out_specs=(pl.BlockSpec((tm, tn), lambda i, j, k: (i, j)),
                   pl.BlockSpec((tm, tn), lambda i, j, k: (i, j))),
        scratch_shapes=[pltpu.VMEM((tm, tn), F32), pltpu.VMEM((tm, tn), F32)],
        compiler_params=_params("parallel", "parallel", "arbitrary"),
        name="dft_half",
    )(g_top, g_top, ab, ab)


def _rope_rotate(v):
    quarter = MLA_ROPE // 4
    up = pltpu.roll(v, LANES - quarter, axis=1)
    down = pltpu.roll(v, quarter, axis=1)
    lane = lax.broadcasted_iota(jnp.int32, v.shape, 1)
    first = ((lane - MLA_NOPE) // quarter) % 2 == 0
    return jnp.where(first, -up, down)


def _proj_kernel(x_ref, sc_ref, sh_ref, g_ref, win_ref, cw_ref, qg_ref, wq_ref, kvg_ref, wk_ref, wvt_ref,
                 cosq_ref, sinq_ref, cosk_ref, sink_ref, ab_ref, cp_ref, q_ref, k_ref, vt_ref, alt_ref):
    gain = g_ref[...] * (1.0 + sc_ref[0])
    shift = sh_ref[0]
    t = x_ref.shape[1]
    sub = min(t, PROJ_SUB_ROWS)
    for r0 in range(0, t, sub):
        rows = slice(r0, r0 + sub)
        hb = (_rms(x_ref[0, rows]) * gain + shift).astype(BF16)

        cq = _dot(hb, win_ref[:, OFF_MLA_Q:OFF_MLA_KV])
        ckv = _dot(hb, win_ref[:, OFF_MLA_KV:OFF_MLA_KR])
        u = _dot(hb, win_ref[:, :OFF_CONV])
        kp1 = _dot(hb, win_ref[:, OFF_MLA_KR:EXT_COLS])
        cp_ref[0, rows] = _dot(hb, win_ref[:, OFF_CONV:OFF_MLA_Q])

        cqn = (_rms(cq) * qg_ref[...]).astype(BF16)
        qq = _dot(cqn, wq_ref[...])
        cosq = cosq_ref[rows]
        sinq = sinq_ref[rows]
        for hd in range(MLA_HEADS):
            lo = hd * HEAD_PAD
            qh = qq[:, lo:lo + HEAD_PAD]
            q_ref[0, rows, lo:lo + HEAD_PAD] = (qh * cosq + _rope_rotate(qh) * sinq).astype(BF16)

        ab = _dot(u.astype(BF16), cw_ref[...])
        ab_ref[0, rows] = ab[:, :GROUP_W].astype(BF16)
        ab_ref[1, rows] = ab[:, GROUP_W:].astype(BF16)
        even_row = lax.broadcasted_iota(jnp.int32, (sub, GROUP_W), 0) % 2 == 0
        alt_j = jnp.sum(jnp.where(even_row, ab[:, :GROUP_W], -ab[:, :GROUP_W]), axis=0, keepdims=True)
        alt = alt_j if r0 == 0 else alt + alt_j

        ckvn = (_rms(ckv) * kvg_ref[...]).astype(BF16)
        kn = _dot(ckvn, wk_ref[...])
        kp = kp1 * cosk_ref[rows] + _rope_rotate(kp1) * sink_ref[rows]
        for hd in range(MLA_HEADS):
            lo = hd * HEAD_PAD
            k_ref[0, rows, lo:lo + HEAD_PAD] = (kn[:, lo:lo + HEAD_PAD] + kp).astype(BF16)
        vt_ref[0, :, rows] = lax.dot_general(wvt_ref[...], ckvn, _NT, preferred_element_type=F32).astype(BF16)
    alt_ref[0, 0] = jnp.broadcast_to(alt, alt_ref.shape[2:])


def _proj(x, scale, shift, g, win, cw, qg, wq, kvg, wk, wvt, cosq, sinq, cosk, sink, t):
    b, n, d = x.shape
    row = lambda bi, i: (bi, 0, 0)
    tile = lambda bi, i: (bi, i, 0)
    tab = pl.BlockSpec((t, LANES), lambda bi, i: (i, 0))
    qk_w = MLA_HEADS * HEAD_PAD
    return pl.pallas_call(
        _proj_kernel,
        out_shape=(jax.ShapeDtypeStruct((2, n, b * GROUP_W), BF16),
                   jax.ShapeDtypeStruct((b, n, 4 * GROUP_W), F32),
                   jax.ShapeDtypeStruct((b, n, qk_w), BF16),
                   jax.ShapeDtypeStruct((b, n, qk_w), BF16),
                   jax.ShapeDtypeStruct((b, GROUP_W, n), BF16),
                   jax.ShapeDtypeStruct((b, n // t, SUBLANES, GROUP_W), F32)),
        grid=(b, n // t),
        in_specs=[pl.BlockSpec((1, t, d), tile),
                  pl.BlockSpec((1, 1, d), row),
                  pl.BlockSpec((1, 1, d), row),
                  _const_spec(g.shape), _const_spec(win.shape), _const_spec(cw.shape),
                  _const_spec(qg.shape), _const_spec(wq.shape), _const_spec(kvg.shape),
                  _const_spec(wk.shape), _const_spec(wvt.shape), tab, tab, tab, tab],
        out_specs=(pl.BlockSpec((2, t, GROUP_W), lambda bi, i: (0, i, bi)),
                   pl.BlockSpec((1, t, 4 * GROUP_W), tile),
                   pl.BlockSpec((1, t, qk_w), tile),
                   pl.BlockSpec((1, t, qk_w), tile),
                   pl.BlockSpec((1, GROUP_W, t), lambda bi, i: (bi, 0, i)),
                   pl.BlockSpec((1, 1, SUBLANES, GROUP_W), lambda bi, i: (bi, i, 0, 0))),
        compiler_params=_params("parallel", "parallel"),
        name="proj",
    )(x, scale, shift, g, win, cw, qg, wq, kvg, wk, wvt, cosq, sinq, cosk, sink)


def _key_segments(k_refs):
    segs = []
    off = 0
    for j, k_ref in enumerate(k_refs):
        nk = k_ref.shape[1]
        size = min(nk, ATTN_SEG)
        for start in range(0, nk, size):
            segs.append((j, start, size, off + start))
        off += nk
    return segs


def _seg_scores(q_ref, k_refs, hd, seg, s_ref):
    j, start, size, off = seg
    lo = hd * HEAD_PAD
    s = lax.dot_general(k_refs[j][0, start:start + size, lo:lo + HEAD_PAD], q_ref[0, :, lo:lo + HEAD_PAD], _NT,
                        preferred_element_type=F32)
    s_ref[hd, off:off + size] = s
    return jnp.max(s, axis=0, keepdims=True)


def _seg_softmax_pv(s_ref, m, vt_refs, hd, seg):
    j, start, size, off = seg
    p = jnp.exp2(s_ref[hd, off:off + size] - m)
    return (jnp.sum(p, axis=0, keepdims=True),
            _dot(vt_refs[j][0, hd * MLA_V:(hd + 1) * MLA_V, start:start + size], p.astype(BF16)))


def _head_pipeline(q_ref, k_refs, s_new, m_new, s_old, m_old, vt_refs, hd):
    m_prev = None if s_old is None else m_old[hd]
    m = l = o = None
    for seg in _key_segments(k_refs):
        if s_new is not None:
            mj = _seg_scores(q_ref, k_refs, hd, seg, s_new)
            m = mj if m is None else jnp.maximum(m, mj)
        if s_old is not None:
            lj, oj = _seg_softmax_pv(s_old, m_prev, vt_refs, hd, seg)
            l = lj if l is None else l + lj
            o = oj if o is None else o + oj
    if s_new is not None:
        m_new[hd] = m
    return None if s_old is None else o / l


def _store_heads(outs, o_ref):
    o_ref[0] = jnp.concatenate(outs, axis=0).T.astype(o_ref.dtype)


def _tile_scratch(n_keys, tq):
    return [pltpu.VMEM((MLA_HEADS, n_keys, tq), F32), pltpu.VMEM((MLA_HEADS, 1, tq), F32)]


def _attn_kernel(n_sets, q_ref, *refs):
    o_ref, s_ref, m_ref = refs[2 * n_sets:]
    k_refs, vt_refs = refs[:n_sets], refs[n_sets:2 * n_sets]
    for hd in range(MLA_HEADS):
        _head_pipeline(q_ref, k_refs, s_ref, m_ref, None, None, None, hd)
    _store_heads([_head_pipeline(None, k_refs, None, None, s_ref, m_ref, vt_refs, hd) for hd in range(MLA_HEADS)],
                 o_ref)


def _attn(q, kv_sets, tq):
    b, n, qk_w = q.shape
    whole = lambda bi, i: (bi, 0, 0)
    ks = [k for k, _ in kv_sets]
    vts = [vt for _, vt in kv_sets]
    return pl.pallas_call(
        functools.partial(_attn_kernel, len(kv_sets)),
        out_shape=jax.ShapeDtypeStruct((b, n, GROUP_W), BF16),
        grid=(b, n // tq),
        in_specs=([pl.BlockSpec((1, tq, qk_w), lambda bi, i: (bi, i, 0))]
                  + [pl.BlockSpec((1,) + k.shape[1:], whole) for k in ks]
                  + [pl.BlockSpec((1,) + vt.shape[1:], whole) for vt in vts]),
        out_specs=pl.BlockSpec((1, tq, GROUP_W), lambda bi, i: (bi, i, 0)),
        scratch_shapes=_tile_scratch(sum(k.shape[1] for k in ks), tq),
        compiler_params=_params("parallel", "arbitrary"),
        name="attn",
    )(q, *ks, *vts)


def _attn_pipe_kernel(n_sets, qe_ref, qo_ref, *refs):
    k_refs = refs[:n_sets]
    vt_prev = refs[n_sets:2 * n_sets]
    vt_cur = refs[2 * n_sets:3 * n_sets]
    oe_ref, oo_ref, s0, m0, s1, m1 = refs[3 * n_sets:]

    @pl.when(pl.program_id(0) == 0)
    def _():
        s1[...] = jnp.zeros_like(s1)
        m1[...] = jnp.zeros_like(m1)

    _store_heads([_head_pipeline(qe_ref, k_refs, s0, m0, s1, m1, vt_prev, hd) for hd in range(MLA_HEADS)], oo_ref)
    _store_heads([_head_pipeline(qo_ref, k_refs, s1, m1, s0, m0, vt_cur, hd) for hd in range(MLA_HEADS)], oe_ref)


def _attn_pipelined(q, kv_sets, tq):
    b, n, qk_w = q.shape
    nq = n // tq
    pairs = b * nq // 2
    ks = [k for k, _ in kv_sets]
    vts = [vt for _, vt in kv_sets]
    even = lambda j: jnp.minimum(2 * j, 2 * pairs - 2)
    b_cur = lambda j: even(j) // nq
    b_prev = lambda j: jnp.maximum(2 * j - 1, 0) // nq
    cur = lambda j: (jnp.minimum(j, pairs - 1), 0, 0)
    prev = lambda j: (jnp.maximum(j - 1, 0), 0, 0)
    once = pl.Buffered(1)
    out = jax.ShapeDtypeStruct((pairs, tq, GROUP_W), BF16)
    out_block = (1, tq, GROUP_W)
    n_keys = sum(k.shape[1] for k in ks)
    even_tiles, odd_tiles = pl.pallas_call(
        functools.partial(_attn_pipe_kernel, len(kv_sets)),
        out_shape=(out, out),
        grid=(pairs + 1,),
        in_specs=([pl.BlockSpec((1, tq, qk_w), lambda j: (b_cur(j), even(j) % nq, 0)),
                   pl.BlockSpec((1, tq, qk_w), lambda j: (b_cur(j), even(j) % nq + 1, 0))]
                  + [pl.BlockSpec((1,) + k.shape[1:], lambda j: (b_cur(j), 0, 0)) for k in ks]
                  + [pl.BlockSpec((1,) + vt.shape[1:], lambda j: (b_prev(j), 0, 0), pipeline_mode=once)
                     for vt in vts]
                  + [pl.BlockSpec((1,) + vt.shape[1:], lambda j: (b_cur(j), 0, 0)) for vt in vts]),
        out_specs=(pl.BlockSpec(out_block, cur), pl.BlockSpec(out_block, prev)),
        scratch_shapes=_tile_scratch(n_keys, tq) * 2,
        compiler_params=pltpu.CompilerParams(dimension_semantics=("arbitrary",),
                                             vmem_limit_bytes=ATTN_VMEM_LIMIT),
        name="attn_pipe",
    )(q, q, *ks, *vts, *vts)
    return [[even_tiles], [odd_tiles]]


def _fourier_tile(i, n_tiles, f_refs):
    if len(f_refs) == 1:
        return f_refs[0][...]
    direct_ref, mirrored_ref, mrow_ref, middle_ref, flip_ref = f_refs
    t = direct_ref.shape[0]
    flipped = _dot(flip_ref[...], mirrored_ref[...]).astype(BF16)
    row0 = jnp.where(2 * i == n_tiles, middle_ref[0:1, :], mrow_ref[0:1, :])
    is_row0 = lax.broadcasted_iota(jnp.int32, (t, GROUP_W), 0) == 0
    mirrored = jnp.where(is_row0, row0, flipped)
    return jnp.where(2 * i >= n_tiles, mirrored, direct_ref[...])


def _mixmlp_kernel(final, n_f, x_ref, cp_ref, prev_ref, next_ref, inv_ref, g1_ref, sc2_ref,
                   sh2_ref, g2_ref, convw_ref, poolw_ref, pscale_ref, wout_ref, n2g_ref, w1_ref, w2_ref,
                   fg_ref, *tail):
    f_refs, at_refs, o_ref = tail[:n_f], tail[n_f:-1], tail[-1]
    i = pl.program_id(1)
    t = x_ref.shape[1]
    first = i == 0
    last = i == pl.num_programs(1) - 1
    prev = jnp.where(first, 0.0, prev_ref[0])
    nxt = jnp.where(last, 0.0, next_ref[0])
    cpe = jnp.concatenate([prev, cp_ref[0], nxt], axis=0)
    sub = min(t, MIX_SUB_ROWS)
    ext = sub + 2 * HALO
    inner = slice(HALO, HALO + sub)
    lane = lax.broadcasted_iota(jnp.int32, (sub, LANES), 1)
    lo_half = lane < SUB_W
    per_row = GROUP_W // at_refs[0].shape[2]
    attn = jnp.concatenate([jnp.concatenate([r[0] for r in at_refs[k:k + per_row]], axis=-1)
                            for k in range(0, len(at_refs), per_row)], axis=0)
    fourier = _fourier_tile(i, pl.num_programs(1), f_refs)
    gain2 = n2g_ref[...] * (1.0 + sc2_ref[0])
    chunk = D_FF // MLP_CHUNKS

    def shifted(a, k):
        return pltpu.roll(a, k % ext, axis=0)

    def mixers(r0, st):
        ce = cpe[r0:r0 + ext]
        z = ce[:, GROUP_W:2 * GROUP_W] * ce[:, 2 * GROUP_W:3 * GROUP_W]
        y = shifted(z, 1) * convw_ref[0:1, :] + z * convw_ref[1:2, :] + shifted(z, -1) * convw_ref[2:3, :]
        st["conv"] = (ce[:, :GROUP_W] * y)[inner]
        u = ce[:, 3 * GROUP_W:]
        s2 = shifted(u, 1) + u
        s4 = shifted(s2, 1) + shifted(s2, -1)
        ub = u[:, LANES:]
        s4b = s4[:, LANES:]
        s8b = shifted(s4b, 2) + shifted(s4b, -2)
        s16b = shifted(s8b, 4) + shifted(s8b, -4)
        inv = inv_ref[r0:r0 + sub]
        pa = jnp.where(lo_half, s2[inner, :LANES], s4[inner, :LANES]) * inv[:, :LANES] - u[inner, :LANES]
        pb = jnp.where(lo_half, s8b[inner], s16b[inner]) * inv[:, LANES:] - ub[inner]
        pin = jnp.concatenate([pa, pb], axis=-1).astype(BF16)
        st["pool"] = _dot(pin, poolw_ref[...]) * pscale_ref[...]

    def out_proj(r0, st):
        rows = slice(r0, r0 + sub)
        mix = jnp.concatenate([fourier[rows], st["conv"].astype(BF16), st["pool"].astype(BF16), attn[rows]], axis=-1)
        st["x1"] = x_ref[0, rows] + g1_ref[0] * _dot(mix, wout_ref[...])

    def norm2(r0, st):
        st["h2"] = (_rms(st["x1"]) * gain2 + sh2_ref[0]).astype(BF16)
        st["acc"] = None

    def mlp_chunk(c, st):
        hid = jnp.maximum(_dot(st["h2"], w1_ref[:, c * chunk:(c + 1) * chunk]), 0.0)
        part = _dot((hid * hid).astype(BF16), w2_ref[c * chunk:(c + 1) * chunk, :])
        st["acc"] = part if st["acc"] is None else st["acc"] + part

    def finish(r0, st):
        x2 = st["x1"] + g2_ref[0] * st["acc"]
        if final:
            x2 = _rms(x2) * fg_ref[...]
        o_ref[0, r0:r0 + sub] = x2

    prologue = (mixers, out_proj, norm2)
    states = [dict() for _ in range(0, t, sub)]
    for step in prologue:
        step(0, states[0])
    for bi, st in enumerate(states):
        nxt_r0 = (bi + 1) * sub
        for c in range(MLP_CHUNKS):
            mlp_chunk(c, st)
            if bi + 1 < len(states) and c < len(prologue):
                prologue[c](nxt_r0, states[bi + 1])
        finish(bi * sub, st)


def _mixmlp(x, cp, f, ats, g1, sc2, sh2, g2, convw, poolw, pscale, wout, n2g, w1, w2, fg, t, final):
    b, n, d = x.shape
    if isinstance(f, tuple):
        direct, mirrored, middle = f
        nt, nh, hb8 = n // t, n // t // 2, t // SUBLANES
        r = jnp.arange(t, dtype=jnp.int32)
        flip = ((r[:, None] + r[None, :]) == t).astype(BF16)
        fs = [direct, mirrored, mirrored, middle, flip]
        f_specs = [pl.BlockSpec((t, GROUP_W), lambda bi, i: (jnp.minimum(i, nh - 1), bi)),
                   pl.BlockSpec((t, GROUP_W), lambda bi, i: (jnp.clip(nt - 1 - i, 0, nh - 1), bi)),
                   pl.BlockSpec((SUBLANES, GROUP_W), lambda bi, i: (jnp.clip((nt - i) * hb8, 0, nh * hb8 - 1), bi)),
                   pl.BlockSpec((SUBLANES, GROUP_W), lambda bi, i: (0, bi)),
                   _const_spec(flip.shape)]
    else:
        fs = [f]
        f_specs = [pl.BlockSpec((t, GROUP_W), lambda bi, i: (i, bi))]
    at_specs = [pl.BlockSpec((1,) + a.shape[1:], lambda bi, i: (bi * (n // t) + i, 0, 0)) for r in ats for a in r]
    ats = [a for r in ats for a in r]
    hb = t // HALO
    nhb = n // HALO
    row = lambda bi, i: (bi, 0, 0)
    tile = lambda bi, i: (bi, i, 0)
    row_spec = pl.BlockSpec((1, 1, d), row)
    return pl.pallas_call(
        functools.partial(_mixmlp_kernel, final, len(fs)),
        out_shape=jax.ShapeDtypeStruct((b, n, d), F32),
        grid=(b, n // t),
        in_specs=[pl.BlockSpec((1, t, d), tile),
                  pl.BlockSpec((1, t, 4 * GROUP_W), tile),
                  pl.BlockSpec((1, HALO, 4 * GROUP_W), lambda bi, i: (bi, jnp.maximum(i * hb - 1, 0), 0)),
                  pl.BlockSpec((1, HALO, 4 * GROUP_W), lambda bi, i: (bi, jnp.minimum((i + 1) * hb, nhb - 1), 0)),
                  pl.BlockSpec((t, GROUP_W), lambda bi, i: (i, 0)),
                  row_spec, row_spec, row_spec, row_spec,
                  _const_spec(convw.shape), _const_spec(poolw.shape), _const_spec(pscale.shape),
                  _const_spec(wout.shape), _const_spec(n2g.shape), _const_spec(w1.shape),
                  _const_spec(w2.shape), _const_spec(fg.shape)] + f_specs + at_specs,
        out_specs=pl.BlockSpec((1, t, d), tile),
        compiler_params=_params("parallel", "arbitrary"),
        name="mixmlp",
    )(x, cp, cp, cp, _pool_inv_counts(n), g1, sc2, sh2, g2, convw, poolw, pscale, wout, n2g, w1, w2, fg, *fs, *ats)


def _layout_weights(w_in, w_uq, w_ukv, pool_w):
    d = w_in.shape[0]
    kr = w_in[:, OFF_MLA_KR:]
    z = lambda r, c: jnp.zeros((r, c), F32)
    pad_lo, pad_hi = MLA_NOPE, HEAD_PAD - MLA_NOPE - MLA_ROPE
    win = jnp.concatenate([w_in[:, :OFF_MLA_KR], z(d, pad_lo), kr, z(d, pad_hi)], axis=1)
    qh = w_uq.reshape(MLA_Q_RANK, MLA_HEADS, MLA_NOPE + MLA_ROPE)
    wq = jnp.pad(qh, ((0, 0), (0, 0), (0, pad_hi))).reshape(MLA_Q_RANK, -1)
    kvh =w_ukv.reshape(MLA_KV_RANK, MLA_HEADS, MLA_NOPE + MLA_V)
    k_plain = jnp.pad(kvh[..., :MLA_NOPE], ((0, 0), (0, 0), (0, HEAD_PAD - MLA_NOPE)))
    wk = k_plain.reshape(MLA_KV_RANK, -1)
    wvt = kvh[..., MLA_NOPE:].reshape(MLA_KV_RANK, -1).T
    poolw = jax.scipy.linalg.block_diag(*[pool_w[g] for g in range(N_SUB)])
    return win.astype(BF16), wq.astype(BF16), wk.astype(BF16), wvt.astype(BF16), poolw.astype(BF16)


def _pool_inv_counts(n):
    tok = jnp.arange(n, dtype=jnp.int32)[:, None]
    half = jnp.repeat(jnp.asarray(POOL_WINDOWS, jnp.int32) // 2, SUB_W)[None, :]
    cnt = jnp.minimum(tok + half - 1, n - 1) - jnp.maximum(tok - half, 0) + 1
    return 1.0 / cnt.astype(F32)


def _rope_tables(n):
    rows = n // GRID_W
    row = jnp.repeat(jnp.arange(rows, dtype=F32), GRID_W)
    col = jnp.tile(jnp.arange(GRID_W, dtype=F32), rows)
    half = MLA_ROPE // 2
    inv = ROPE_BASE ** (-jnp.arange(0, half, 2, dtype=F32) / half)
    ang_r = row[:, None] * inv[None, :]
    ang_c = col[:, None] * inv[None, :]
    ang = jnp.concatenate([ang_r, ang_r, ang_c, ang_c], axis=-1)
    return _place_tables(jnp.cos(ang), jnp.sin(ang))


def _place_tables(cos, sin):
    n = cos.shape[0]
    qs = MLA_SCALE * LOG2E
    pad_hi = HEAD_PAD - MLA_NOPE - MLA_ROPE
    zl, zh = jnp.zeros((n, MLA_NOPE), F32), jnp.zeros((n, pad_hi), F32)
    cosq = jnp.concatenate([jnp.full((n, MLA_NOPE), qs, F32), cos * qs, zh], axis=1)
    sinq = jnp.concatenate([zl, sin * qs, zh], axis=1)
    cosk = jnp.concatenate([zl, cos, zh], axis=1)
    sink = jnp.concatenate([zl, sin, zh], axis=1)
    return cosq, sinq, cosk, sink


def _tile(n, want):
    return min(n, want)


def kernel(x, c, ctx, c_ctx, ada_w, ada_b, norm1_g, norm2_g, w_in, fourier_w, conv_w, pool_w, pool_scale,
           q_norm_g, w_uq, kv_norm_g, w_ukv, w_out, mlp_w1, mlp_w2, final_norm_g):
    b, n, d = x.shape
    nc = ctx.shape[1]
    depth = ada_w.shape[0]

    rows = -(-(b + 1) // SUBLANES) * SUBLANES
    cc = jnp.concatenate([c, c_ctx[None, :], jnp.zeros((rows - b - 1, d), F32)], axis=0)
    mod = _ada(cc, ada_w, ada_b)
    cw = _fold_fourier(fourier_w)
    half_dft = n >= 2 * TOKEN_TILE
    g_x = _dft_matrix(n, n // 2 if half_dft else n)
    g_c = _dft_matrix(nc, nc)
    tabs_x = _rope_tables(n)
    tabs_c = _place_tables(jnp.ones((nc, MLA_ROPE), F32), jnp.zeros((nc, MLA_ROPE), F32))
    fg = final_norm_g.reshape(1, d)

    def stream_mod(l, ctx_stream):
        m = mod[l, b:b + 1] if ctx_stream else mod[l, :b]
        m = jnp.broadcast_to(m, (b, 6 * d)).reshape(b, 1, 6, d)
        return [m[:, :, j, :] for j in range(6)]

    for l in range(depth):
        last = l == depth - 1
        win, wq, wk, wvt, poolw = _layout_weights(w_in[l], w_uq[l], w_ukv[l], pool_w[l])
        n1g = norm1_g[l].reshape(1, d)
        n2g = norm2_g[l].reshape(1, d)
        qg = q_norm_g[l].reshape(1, -1)
        kvg = kv_norm_g[l].reshape(1, -1)
        pscale = pool_scale[l].reshape(1, -1)
        wout = w_out[l].astype(BF16)
        w1 = mlp_w1[l].astype(BF16)
        w2 = mlp_w2[l].astype(BF16)

        def project(tokens, ctx_stream, tabs):
            sh1, sc1 = stream_mod(l, ctx_stream)[:2]
            t = _tile(tokens.shape[1], PROJ_TILE)
            return _proj(tokens, sc1, sh1, n1g, win, cw[l], qg, wq, kvg, wk, wvt, *tabs, t)

        def finish(tokens, ctx_stream, cp, ab, alt, g, q, kv_sets, final):
            nt = tokens.shape[1]
            t = _tile(nt, TOKEN_TILE)
            _, _, g1, sh2, sc2, g2 = stream_mod(l, ctx_stream)
            cols = b * GROUP_W
            if g.shape[0] < nt:
                direct, mirrored = _dft_half(g, ab, _tile(nt // 2, 1024), _tile(cols, 1024), _tile(nt, 2048))
                middle = jnp.sum(alt[:, :, 0, :], axis=1).reshape(1, cols) * (1.0 / math.sqrt(nt))
                f = (direct, mirrored, jnp.broadcast_to(middle, (SUBLANES, cols)).astype(BF16))
            else:
                f = _matmul(g, ab.reshape(2 * nt, cols), _tile(nt, 1024), _tile(cols, 2048), _tile(2 * nt, 1024))
            if t == 2 * ATTN_TILE:
                ats = _attn_pipelined(q, kv_sets, ATTN_TILE)
            else:
                ats = [[_attn(q, kv_sets, t).reshape(b * nt // t, t, GROUP_W)]]
            return _mixmlp(tokens, cp, f, ats, g1, sc2, sh2, g2, conv_w[l], poolw, pscale, wout, n2g, w1, w2,
                           fg, t, final)

        ab_c, cp_c, q_c, k_c, vt_c, alt_c = project(ctx, True, tabs_c)
        ab_x, cp_x, q_x, k_x, vt_x, alt_x = project(x, False, tabs_x)
        x = finish(x, False, cp_x, ab_x, alt_x, g_x, q_x, [(k_x, vt_x), (k_c, vt_c)], last)
        if not last:
            ctx = finish(ctx, True, cp_c, ab_c, alt_c, g_c, q_c, [(k_c, vt_c)], False)
    return x
```

```python
import functools
import math

import jax
import jax.numpy as jnp
from jax import lax
from jax.experimental import pallas as pl
from jax.experimental.pallas import tpu as pltpu

D_MODEL = 1024
DEPTH = 2
GRID_W = 64
GROUP_W = D_MODEL // 4
N_SUB = 4
SUB_W = GROUP_W // N_SUB
POOL_WINDOWS = (2, 4, 8, 16)
MLA_HEADS = 4
MLA_NOPE = GROUP_W // MLA_HEADS
MLA_ROPE = MLA_NOPE // 2
MLA_V = GROUP_W // MLA_HEADS
MLA_Q_RANK = GROUP_W
MLA_KV_RANK = GROUP_W // 2
MLA_SCALE = 1.0 / math.sqrt(MLA_NOPE + MLA_ROPE)
ROPE_BASE = 10000.0
D_FF = 4 * D_MODEL
EPS = 1e-6

OFF_CONV = GROUP_W
OFF_POOL = OFF_CONV + 3 * GROUP_W
OFF_MLA_Q = OFF_POOL + GROUP_W
OFF_MLA_KV = OFF_MLA_Q + MLA_Q_RANK
OFF_MLA_KR = OFF_MLA_KV + MLA_KV_RANK
IN_COLS = OFF_MLA_KR + MLA_ROPE

LANES = 128
SUBLANES = 8
HEAD_PAD = LANES
EXT_COLS = OFF_MLA_KR + LANES
DFT_ROWS = 64
HALO = SUBLANES
MLP_CHUNKS = 4
PROJ_SUB_ROWS = 512
VMEM_LIMIT = 52 * 1024 * 1024
ATTN_VMEM_LIMIT = 58 * 1024 * 1024
ATTN_SEG = 2048
ATTN_TILE = 256
TOKEN_TILE = 512
PROJ_TILE = 1024
ADA_COLS = 1536
DFT_TILE = 1024
DFT_WIDE = 2048
LOG2E = math.log2(math.e)

F32 = jnp.float32
BF16 = jnp.bfloat16


_NT = (((1,), (1,)), ((), ()))


def _dot(a, b):
    return jnp.dot(a, b, preferred_element_type=F32)


def _rms(x):
    return x * lax.rsqrt(jnp.mean(x * x, axis=-1, keepdims=True) + EPS)


def _params(*sem):
    return pltpu.CompilerParams(dimension_semantics=sem, vmem_limit_bytes=VMEM_LIMIT)


def _const_spec(shape):
    nd = len(shape)
    return pl.BlockSpec(shape, lambda *_: (0,) * nd, pipeline_mode=pl.Buffered(1))


def _ada_kernel(cc_ref, w_ref, b_ref, o_ref):
    cc = cc_ref[...]
    s = cc * (1.0 / (1.0 + jnp.exp(-cc)))
    o_ref[0] = _dot(s.astype(BF16), w_ref[0].astype(BF16)) + b_ref[0]


def _ada(cc, ada_w, ada_b):
    depth, d, cols = ada_w.shape
    rows = cc.shape[0]
    tn = ADA_COLS
    return pl.pallas_call(
        _ada_kernel,
        out_shape=jax.ShapeDtypeStruct((depth, rows, cols), F32),
        grid=(depth, cols // tn),
        in_specs=[pl.BlockSpec((rows, d), lambda l, j: (0, 0)),
                  pl.BlockSpec((1, d, tn), lambda l, j: (l, 0, j)),
                  pl.BlockSpec((1, 1, tn), lambda l, j: (l, 0, j))],
        out_specs=pl.BlockSpec((1, rows, tn), lambda l, j: (l, 0, j)),
        compiler_params=_params("arbitrary", "arbitrary"),
        name="ada",
    )(cc, ada_w, ada_b.reshape(depth, 1, cols))


def _fold_kernel(dft_ref, w_ref, o_ref):
    w = w_ref[0]
    c = jnp.dot(dft_ref[:, :GROUP_W], w, preferred_element_type=F32, precision=lax.Precision.HIGHEST)
    s = jnp.dot(dft_ref[:, GROUP_W:], w, preferred_element_type=F32, precision=lax.Precision.HIGHEST)
    o_ref[0, :, :GROUP_W] = c.astype(BF16)
    o_ref[0, :, GROUP_W:] = s.astype(BF16)


def _channel_dft():
    i = jnp.arange(GROUP_W, dtype=jnp.int32)
    same = (i[:, None] // SUB_W) == (i[None, :] // SUB_W)
    ang = ((i[:, None] % SUB_W) * (i[None, :] % SUB_W) % SUB_W).astype(F32) * (2.0 * math.pi / SUB_W)
    scale = 1.0 / math.sqrt(SUB_W)
    c = jnp.where(same, jnp.cos(ang), 0.0) * scale
    s = jnp.where(same, jnp.sin(ang), 0.0) * scale
    return jnp.concatenate([c, s], axis=1)


def _fold_fourier(fourier_w):
    depth = fourier_w.shape[0]
    return pl.pallas_call(
        _fold_kernel,
        out_shape=jax.ShapeDtypeStruct((depth, GROUP_W, 2 * GROUP_W), BF16),
        grid=(depth,),
        in_specs=[pl.BlockSpec((GROUP_W, 2 * GROUP_W), lambda l: (0, 0)),
                  pl.BlockSpec((1, GROUP_W, GROUP_W), lambda l: (l, 0, 0))],
        out_specs=pl.BlockSpec((1, GROUP_W, 2 * GROUP_W), lambda l: (l, 0, 0)),
        compiler_params=_params("arbitrary"),
        name="fold_fourier",
    )(_channel_dft(), fourier_w)


def _dftgen_kernel(n, c1_ref, s1_ref, cb_ref, sb_ref, o_ref):
    c1 = c1_ref[0]
    s1 = s1_ref[0]
    cb = cb_ref[...]
    sb = sb_ref[...]
    scale = 1.0 / math.sqrt(n)
    o_ref[:, :n] = ((c1 * cb - s1 * sb) * scale).astype(BF16)
    o_ref[:, n:] = ((s1 * cb + c1 * sb) * (-scale)).astype(BF16)


def _dft_matrix(n, rows):
    nb = rows // DFT_ROWS
    k = jnp.arange(n, dtype=jnp.int32)
    j1 = jnp.arange(nb, dtype=jnp.int32) * DFT_ROWS
    j0 = jnp.arange(DFT_ROWS, dtype=jnp.int32)
    w = 2.0 * math.pi / n
    a1 = ((j1[:, None] * k[None, :]) % n).astype(F32) * w
    a0 = ((j0[:, None] * k[None, :]) % n).astype(F32) * w
    row_spec = pl.BlockSpec((1, 1, n), lambda i: (i, 0, 0))
    tab_spec = pl.BlockSpec((DFT_ROWS, n), lambda i: (0, 0))
    return pl.pallas_call(
        functools.partial(_dftgen_kernel, n),
        out_shape=jax.ShapeDtypeStruct((rows, 2 * n), BF16),
        grid=(nb,),
        in_specs=[row_spec, row_spec, tab_spec, tab_spec],
        out_specs=pl.BlockSpec((DFT_ROWS, 2 * n), lambda i: (i, 0)),
        compiler_params=_params("arbitrary"),
        name="dft_matrix",
    )(jnp.cos(a1).reshape(nb, 1, n), jnp.sin(a1).reshape(nb, 1, n), jnp.cos(a0), jnp.sin(a0))


def _matmul_kernel(a_ref, b_ref, o_ref, acc_ref):
    k = pl.program_id(2)

    @pl.when(k == 0)
    def _():
        acc_ref[...] = jnp.zeros_like(acc_ref)

    acc_ref[...] += _dot(a_ref[...], b_ref[...])

    @pl.when(k == pl.num_programs(2) - 1)
    def _():
        o_ref[...] = acc_ref[...].astype(o_ref.dtype)


def _matmul(a, b, tm, tn, tk):
    m, kk = a.shape
    n = b.shape[1]
    return pl.pallas_call(
        _matmul_kernel,
        out_shape=jax.ShapeDtypeStruct((m, n), BF16),
        grid=(m // tm, n // tn, kk // tk),
        in_specs=[pl.BlockSpec((tm, tk), lambda i, j, k: (i, k)),
                  pl.BlockSpec((tk, tn), lambda i, j, k: (k, j))],
        out_specs=pl.BlockSpec((tm, tn), lambda i, j, k: (i, j)),
        scratch_shapes=[pltpu.VMEM((tm, tn), F32)],
        compiler_params=_params("parallel", "parallel", "arbitrary"),
        name="dft_matmul",
    )(a, b)


def _dft_half_kernel(c_ref, s_ref, a_ref, b_ref, direct_ref, mirrored_ref, accp_ref, accq_ref):
    k = pl.program_id(2)

    @pl.when(k == 0)
    def _():
        accp_ref[...] = jnp.zeros_like(accp_ref)
        accq_ref[...] = jnp.zeros_like(accq_ref)

    accp_ref[...] += _dot(c_ref[...], a_ref[0])
    accq_ref[...] += _dot(s_ref[...], b_ref[0])

    @pl.when(k == pl.num_programs(2) - 1)
    def _():
        direct_ref[...] = (accp_ref[...] + accq_ref[...]).astype(direct_ref.dtype)
        mirrored_ref[...] = (accp_ref[...] - accq_ref[...]).astype(mirrored_ref.dtype)


def _dft_half(g_top, ab, tm, tn, tk):
    h, n2 = g_top.shape
    n = n2 // 2
    cols = ab.shape[2]
    out = jax.ShapeDtypeStruct((h, cols), BF16)
    kb = n // tk
    return pl.pallas_call(
        _dft_half_kernel,
        out_shape=(out, out),
        grid=(h // tm, cols // tn, kb),
        in_specs=[pl.BlockSpec((tm, tk), lambda i, j, k: (i, k)),
                  pl.BlockSpec((tm, tk), lambda i, j, k: (i, k + kb)),
                  pl.BlockSpec((1, tk, tn), lambda i, j, k: (0, k, j)),
                  pl.BlockSpec((1, tk, tn), lambda i, j, k: (1, k, j))],
        out_specs=(pl.BlockSpec((tm, tn), lambda i, j, k: (i, j)),
                   pl.BlockSpec((tm, tn), lambda i, j, k: (i, j))),
        scratch_shapes=[pltpu.VMEM((tm, tn), F32), pltpu.VMEM((tm, tn), F32)],
        compiler_params=_params("parallel", "parallel", "arbitrary"),
        name="dft_half",
    )(g_top, g_top, ab, ab)


def _rope_rotate(v):
    quarter = MLA_ROPE // 4
    up = pltpu.roll(v, LANES - quarter, axis=1)
    down = pltpu.roll(v, quarter, axis=1)
    lane = lax.broadcasted_iota(jnp.int32, v.shape, 1)
    first = ((lane - MLA_NOPE) // quarter) % 2 == 0
    return jnp.where(first, -up, down)


def _proj_kernel(x_ref, sc_ref, sh_ref, g_ref, win_ref, cw_ref, qg_ref, wq_ref, kvg_ref, wk_ref, wvt_ref,
                 cosq_ref, sinq_ref, cosk_ref, sink_ref, ab_ref, cp_ref, q_ref, k_ref, vt_ref, alt_ref):
    gain = g_ref[...] * (1.0 + sc_ref[0])
    shift = sh_ref[0]
    t = x_ref.shape[1]
    sub = min(t, PROJ_SUB_ROWS)
    for r0 in range(0, t, sub):
        rows = slice(r0, r0 + sub)
        hb = (_rms(x_ref[0, rows]) * gain + shift).astype(BF16)

        cq = _dot(hb, win_ref[:, OFF_MLA_Q:OFF_MLA_KV])
        ckv = _dot(hb, win_ref[:, OFF_MLA_KV:OFF_MLA_KR])
        u = _dot(hb, win_ref[:, :OFF_CONV])
        kp1 = _dot(hb, win_ref[:, OFF_MLA_KR:EXT_COLS])
        cp_ref[0, rows] = _dot(hb, win_ref[:, OFF_CONV:OFF_MLA_Q])

        cqn = (_rms(cq) * qg_ref[...]).astype(BF16)
        qq = _dot(cqn, wq_ref[...])
        cosq = cosq_ref[rows]
        sinq = sinq_ref[rows]
        for hd in range(MLA_HEADS):
            lo = hd * HEAD_PAD
            qh = qq[:, lo:lo + HEAD_PAD]
            q_ref[0, rows, lo:lo + HEAD_PAD] = (qh * cosq + _rope_rotate(qh) * sinq).astype(BF16)

        ab = _dot(u.astype(BF16), cw_ref[...])
        ab_ref[0, rows] = ab[:, :GROUP_W].astype(BF16)
        ab_ref[1, rows] = ab[:, GROUP_W:].astype(BF16)
        even_row = lax.broadcasted_iota(jnp.int32, (sub, GROUP_W), 0) % 2 == 0
        alt_j = jnp.sum(jnp.where(even_row, ab[:, :GROUP_W], -ab[:, :GROUP_W]), axis=0, keepdims=True)
        alt = alt_j if r0 == 0 else alt + alt_j

        ckvn = (_rms(ckv) * kvg_ref[...]).astype(BF16)
        kn = _dot(ckvn, wk_ref[...])
        kp = kp1 * cosk_ref[rows] + _rope_rotate(kp1) * sink_ref[rows]
        for hd in range(MLA_HEADS):
            lo = hd * HEAD_PAD
            k_ref[0, rows, lo:lo + HEAD_PAD] = (kn[:, lo:lo + HEAD_PAD] + kp).astype(BF16)
        vt_ref[0, :, rows] = lax.dot_general(wvt_ref[...], ckvn, _NT, preferred_element_type=F32).astype(BF16)
    alt_ref[0, 0] = jnp.broadcast_to(alt, alt_ref.shape[2:])


def _proj(x, scale, shift, g, win, cw, qg, wq, kvg, wk, wvt, cosq, sinq, cosk, sink, t):
    b, n, d = x.shape
    row = lambda bi, i: (bi, 0, 0)
    tile = lambda bi, i: (bi, i, 0)
    tab = pl.BlockSpec((t, LANES), lambda bi, i: (i, 0))
    qk_w = MLA_HEADS * HEAD_PAD
    return pl.pallas_call(
        _proj_kernel,
        out_shape=(jax.ShapeDtypeStruct((2, n, b * GROUP_W), BF16),
                   jax.ShapeDtypeStruct((b, n, 4 * GROUP_W), F32),
                   jax.ShapeDtypeStruct((b, n, qk_w), BF16),
                   jax.ShapeDtypeStruct((b, n, qk_w), BF16),
                   jax.ShapeDtypeStruct((b, GROUP_W, n), BF16),
                   jax.ShapeDtypeStruct((b, n // t, SUBLANES, GROUP_W), F32)),
        grid=(b, n // t),
        in_specs=[pl.BlockSpec((1, t, d), tile),
                  pl.BlockSpec((1, 1, d), row),
                  pl.BlockSpec((1, 1, d), row),
                  _const_spec(g.shape), _const_spec(win.shape), _const_spec(cw.shape),
                  _const_spec(qg.shape), _const_spec(wq.shape), _const_spec(kvg.shape),
                  _const_spec(wk.shape), _const_spec(wvt.shape), tab, tab, tab, tab],
        out_specs=(pl.BlockSpec((2, t, GROUP_W), lambda bi, i: (0, i, bi)),
                   pl.BlockSpec((1, t, 4 * GROUP_W), tile),
                   pl.BlockSpec((1, t, qk_w), tile),
                   pl.BlockSpec((1, t, qk_w), tile),
                   pl.BlockSpec((1, GROUP_W, t), lambda bi, i: (bi, 0, i)),
                   pl.BlockSpec((1, 1, SUBLANES, GROUP_W), lambda bi, i: (bi, i, 0, 0))),
        compiler_params=_params("parallel", "parallel"),
        name="proj",
    )(x, scale, shift, g, win, cw, qg, wq, kvg, wk, wvt, cosq, sinq, cosk, sink)


def _key_segments(k_refs):
    segs = []
    off = 0
    for j, k_ref in enumerate(k_refs):
        nk = k_ref.shape[1]
        size = min(nk, ATTN_SEG)
        for start in range(0, nk, size):
            segs.append((j, start, size, off + start))
        off += nk
    return segs


def _seg_scores(q_ref, k_refs, hd, seg, s_ref):
    j, start, size, off = seg
    lo = hd * HEAD_PAD
    s = lax.dot_general(k_refs[j][0, start:start + size, lo:lo + HEAD_PAD], q_ref[0, :, lo:lo + HEAD_PAD], _NT,
                        preferred_element_type=F32)
    s_ref[hd, off:off + size] = s
    return jnp.max(s, axis=0, keepdims=True)


def _seg_softmax_pv(s_ref, m, vt_refs, hd, seg):
    j, start, size, off = seg
    p = jnp.exp2(s_ref[hd, off:off + size] - m)
    return (jnp.sum(p, axis=0, keepdims=True),
            _dot(vt_refs[j][0, hd * MLA_V:(hd + 1) * MLA_V, start:start + size], p.astype(BF16)))


def _head_pipeline(q_ref, k_refs, s_new, m_new, s_old, m_old, vt_refs, hd):
    m_prev = None if s_old is None else m_old[hd]
    m = l = o = None
    for seg in _key_segments(k_refs):
        if s_new is not None:
            mj = _seg_scores(q_ref, k_refs, hd, seg, s_new)
            m = mj if m is None else jnp.maximum(m, mj)
        if s_old is not None:
            lj, oj = _seg_softmax_pv(s_old, m_prev, vt_refs, hd, seg)
            l = lj if l is None else l + lj
            o = oj if o is None else o + oj
    if s_new is not None:
        m_new[hd] = m
    return None if s_old is None else o / l


def _store_heads(outs, o_ref):
    o_ref[0] = jnp.concatenate(outs, axis=0).T.astype(o_ref.dtype)


def _tile_scratch(n_keys, tq):
    return [pltpu.VMEM((MLA_HEADS, n_keys, tq), F32), pltpu.VMEM((MLA_HEADS, 1, tq), F32)]


def _attn_kernel(n_sets, q_ref, *refs):
    o_ref, s_ref, m_ref = refs[2 * n_sets:]
    k_refs, vt_refs = refs[:n_sets], refs[n_sets:2 * n_sets]
    for hd in range(MLA_HEADS):
        _head_pipeline(q_ref, k_refs, s_ref, m_ref, None, None, None, hd)
    _store_heads([_head_pipeline(None, k_refs, None, None, s_ref, m_ref, vt_refs, hd) for hd in range(MLA_HEADS)],
                 o_ref)


def _attn(q, kv_sets, tq):
    b, n, qk_w = q.shape
    whole = lambda bi, i: (bi, 0, 0)
    ks = [k for k, _ in kv_sets]
    vts = [vt for _, vt in kv_sets]
    return pl.pallas_call(
        functools.partial(_attn_kernel, len(kv_sets)),
        out_shape=jax.ShapeDtypeStruct((b, n, GROUP_W), BF16),
        grid=(b, n // tq),
        in_specs=([pl.BlockSpec((1, tq, qk_w), lambda bi, i: (bi, i, 0))]
                  + [pl.BlockSpec((1,) + k.shape[1:], whole) for k in ks]
                  + [pl.BlockSpec((1,) + vt.shape[1:], whole) for vt in vts]),
        out_specs=pl.BlockSpec((1, tq, GROUP_W), lambda bi, i: (bi, i, 0)),
        scratch_shapes=_tile_scratch(sum(k.shape[1] for k in ks), tq),
        compiler_params=_params("parallel", "arbitrary"),
        name="attn",
    )(q, *ks, *vts)


def _attn_pipe_kernel(n_sets, qe_ref, qo_ref, *refs):
    k_refs = refs[:n_sets]
    vt_prev = refs[n_sets:2 * n_sets]
    vt_cur = refs[2 * n_sets:3 * n_sets]
    oe_ref, oo_ref, s0, m0, s1, m1 = refs[3 * n_sets:]

    @pl.when(pl.program_id(0) == 0)
    def _():
        s1[...] = jnp.zeros_like(s1)
        m1[...] = jnp.zeros_like(m1)

    _store_heads([_head_pipeline(qe_ref, k_refs, s0, m0, s1, m1, vt_prev, hd) for hd in range(MLA_HEADS)], oo_ref)
    _store_heads([_head_pipeline(qo_ref, k_refs, s1, m1, s0, m0, vt_cur, hd) for hd in range(MLA_HEADS)], oe_ref)


def _attn_pipelined(q, kv_sets, tq):
    b, n, qk_w = q.shape
    nq = n // tq
    pairs = b * nq // 2
    ks = [k for k, _ in kv_sets]
    vts = [vt for _, vt in kv_sets]
    even = lambda j: jnp.minimum(2 * j, 2 * pairs - 2)
    b_cur = lambda j: even(j) // nq
    b_prev = lambda j: jnp.maximum(2 * j - 1, 0) // nq
    cur = lambda j: (jnp.minimum(j, pairs - 1), 0, 0)
    prev = lambda j: (jnp.maximum(j - 1, 0), 0, 0)
    out = jax.ShapeDtypeStruct((pairs, tq, GROUP_W), BF16)
    out_block = (1, tq, GROUP_W)
    n_keys = sum(k.shape[1] for k in ks)
    even_tiles, odd_tiles = pl.pallas_call(
        functools.partial(_attn_pipe_kernel, len(kv_sets)),
        out_shape=(out, out),
        grid=(pairs + 1,),
        in_specs=([pl.BlockSpec((1, tq, qk_w), lambda j: (b_cur(j), even(j) % nq, 0)),
                   pl.BlockSpec((1, tq, qk_w), lambda j: (b_cur(j), even(j) % nq + 1, 0))]
                  + [pl.BlockSpec((1,) + k.shape[1:], lambda j: (b_cur(j), 0, 0)) for k in ks]
                  + [pl.BlockSpec((1,) + vt.shape[1:], lambda j: (b_prev(j), 0, 0)) for vt in vts]
                  + [pl.BlockSpec((1,) + vt.shape[1:], lambda j: (b_cur(j), 0, 0)) for vt in vts]),
        out_specs=(pl.BlockSpec(out_block, cur), pl.BlockSpec(out_block, prev)),
        scratch_shapes=_tile_scratch(n_keys, tq) * 2,
        compiler_params=pltpu.CompilerParams(dimension_semantics=("arbitrary",),
                                             vmem_limit_bytes=ATTN_VMEM_LIMIT),
        name="attn_pipe",
    )(q, q, *ks, *vts, *vts)
    return [[even_tiles], [odd_tiles]]


def _fourier_tile(i, n_tiles, f_refs):
    if len(f_refs) == 1:
        return f_refs[0][...]
    direct_ref, mirrored_ref, mrow_ref, middle_ref, flip_ref = f_refs
    t = direct_ref.shape[0]
    flipped = _dot(flip_ref[...], mirrored_ref[...]).astype(BF16)
    row0 = jnp.where(2 * i == n_tiles, middle_ref[0:1, :], mrow_ref[0:1, :])
    is_row0 = lax.broadcasted_iota(jnp.int32, (t, GROUP_W), 0) == 0
    mirrored = jnp.where(is_row0, row0, flipped)
    return jnp.where(2 * i >= n_tiles, mirrored, direct_ref[...])


def _mixmlp_kernel(final, n_f, x_ref, cp_ref, prev_ref, next_ref, inv_ref, g1_ref, sc2_ref,
                   sh2_ref, g2_ref, convw_ref, poolw_ref, pscale_ref, wout_ref, n2g_ref, w1_ref, w2_ref,
                   fg_ref, *tail):
    f_refs, at_refs, o_ref = tail[:n_f], tail[n_f:-1], tail[-1]
    i = pl.program_id(1)
    t = x_ref.shape[1]
    first = i == 0
    last = i == pl.num_programs(1) - 1
    prev = jnp.where(first, 0.0, prev_ref[0])
    nxt = jnp.where(last, 0.0, next_ref[0])
    cpe = jnp.concatenate([prev, cp_ref[0], nxt], axis=0)
    ext = t + 2 * HALO
    inner = slice(HALO, HALO + t)
    lo_half = lax.broadcasted_iota(jnp.int32, (t, LANES), 1) < SUB_W
    per_row = GROUP_W // at_refs[0].shape[2]
    attn = jnp.concatenate([jnp.concatenate([r[0] for r in at_refs[k:k + per_row]], axis=-1)
                            for k in range(0, len(at_refs), per_row)], axis=0)
    fourier = _fourier_tile(i, pl.num_programs(1), f_refs)

    def shifted(a, k):
        return pltpu.roll(a, k % ext, axis=0)

    z = cpe[:, GROUP_W:2 * GROUP_W] * cpe[:, 2 * GROUP_W:3 * GROUP_W]
    y = shifted(z, 1) * convw_ref[0:1, :] + z * convw_ref[1:2, :] + shifted(z, -1) * convw_ref[2:3, :]
    conv = (cpe[:, :GROUP_W] * y)[inner]

    u = cpe[:, 3 * GROUP_W:]
    s2 = shifted(u, 1) + u
    s4 = shifted(s2, 1) + shifted(s2, -1)
    ub = u[:, LANES:]
    s4b = s4[:, LANES:]
    s8b = shifted(s4b, 2) + shifted(s4b, -2)
    s16b = shifted(s8b, 4) + shifted(s8b, -4)
    inv = inv_ref[...]
    pa = jnp.where(lo_half, s2[inner, :LANES], s4[inner, :LANES]) * inv[:, :LANES] - u[inner, :LANES]
    pb = jnp.where(lo_half, s8b[inner], s16b[inner]) * inv[:, LANES:] - ub[inner]
    pin = jnp.concatenate([pa, pb], axis=-1).astype(BF16)
    pool = _dot(pin, poolw_ref[...]) * pscale_ref[...]

    mix = jnp.concatenate([fourier, conv.astype(BF16), pool.astype(BF16), attn], axis=-1)
    x1 = x_ref[0] + g1_ref[0] * _dot(mix, wout_ref[...])

    h2 = (_rms(x1) * (n2g_ref[...] * (1.0 + sc2_ref[0])) + sh2_ref[0]).astype(BF16)
    chunk = D_FF // MLP_CHUNKS
    acc = None
    for c in range(MLP_CHUNKS):
        hid = jnp.maximum(_dot(h2, w1_ref[:, c * chunk:(c + 1) * chunk]), 0.0)
        part = _dot((hid * hid).astype(BF16), w2_ref[c * chunk:(c + 1) * chunk, :])
        acc = part if acc is None else acc + part
    x2 = x1 + g2_ref[0] * acc
    if final:
        x2 = _rms(x2) * fg_ref[...]
    o_ref[0] = x2


def _mixmlp(x, cp, f, ats, g1, sc2, sh2, g2, convw, poolw, pscale, wout, n2g, w1, w2, fg, t, final):
    b, n, d = x.shape
    if isinstance(f, tuple):
        direct, mirrored, middle = f
        nt, nh, hb8 = n // t, n // t // 2, t // SUBLANES
        r = jnp.arange(t, dtype=jnp.int32)
        flip = ((r[:, None] + r[None, :]) == t).astype(BF16)
        fs = [direct, mirrored, mirrored, middle, flip]
        f_specs = [pl.BlockSpec((t, GROUP_W), lambda bi, i: (jnp.minimum(i, nh - 1), bi)),
                   pl.BlockSpec((t, GROUP_W), lambda bi, i: (jnp.clip(nt - 1 - i, 0, nh - 1), bi)),
                   pl.BlockSpec((SUBLANES, GROUP_W), lambda bi, i: (jnp.clip((nt - i) * hb8, 0, nh * hb8 - 1), bi)),
                   pl.BlockSpec((SUBLANES, GROUP_W), lambda bi, i: (0, bi)),
                   _const_spec(flip.shape)]
    else:
        fs = [f]
        f_specs = [pl.BlockSpec((t, GROUP_W), lambda bi, i: (i, bi))]
    at_specs = [pl.BlockSpec((1,) + a.shape[1:], lambda bi, i: (bi * (n // t) + i, 0, 0)) for r in ats for a in r]
    ats = [a for r in ats for a in r]
    hb = t // HALO
    nhb = n // HALO
    row = lambda bi, i: (bi, 0, 0)
    tile = lambda bi, i: (bi, i, 0)
    row_spec = pl.BlockSpec((1, 1, d), row)
    return pl.pallas_call(
        functools.partial(_mixmlp_kernel, final, len(fs)),
        out_shape=jax.ShapeDtypeStruct((b, n, d), F32),
        grid=(b, n // t),
        in_specs=[pl.BlockSpec((1, t, d), tile),
                  pl.BlockSpec((1, t, 4 * GROUP_W), tile),
                  pl.BlockSpec((1, HALO, 4 * GROUP_W), lambda bi, i: (bi, jnp.maximum(i * hb - 1, 0), 0)),
                  pl.BlockSpec((1, HALO, 4 * GROUP_W), lambda bi, i: (bi, jnp.minimum((i + 1) * hb, nhb - 1), 0)),
                  pl.BlockSpec((t, GROUP_W), lambda bi, i: (i, 0)),
                  row_spec, row_spec, row_spec, row_spec,
                  _const_spec(convw.shape), _const_spec(poolw.shape), _const_spec(pscale.shape),
                  _const_spec(wout.shape), _const_spec(n2g.shape), _const_spec(w1.shape),
                  _const_spec(w2.shape), _const_spec(fg.shape)] + f_specs + at_specs,
        out_specs=pl.BlockSpec((1, t, d), tile),
        compiler_params=_params("parallel", "arbitrary"),
        name="mixmlp",
    )(x, cp, cp, cp, _pool_inv_counts(n), g1, sc2, sh2, g2, convw, poolw, pscale, wout, n2g, w1, w2, fg, *fs, *ats)


def _layout_weights(w_in, w_uq, w_ukv, pool_w):
    d = w_in.shape[0]
    kr = w_in[:, OFF_MLA_KR:]
    z = lambda r, c: jnp.zeros((r, c), F32)
    pad_lo, pad_hi = MLA_NOPE, HEAD_PAD - MLA_NOPE - MLA_ROPE
    win = jnp.concatenate([w_in[:, :OFF_MLA_KR], z(d, pad_lo), kr, z(d, pad_hi)], axis=1)
    qh = w_uq.reshape(MLA_Q_RANK, MLA_HEADS, MLA_NOPE + MLA_ROPE)
    wq = jnp.pad(qh, ((0, 0), (0, 0), (0, pad_hi))).reshape(MLA_Q_RANK, -1)
    kvh =w_ukv.reshape(MLA_KV_RANK, MLA_HEADS, MLA_NOPE + MLA_V)
    k_plain = jnp.pad(kvh[..., :MLA_NOPE], ((0, 0), (0, 0), (0, HEAD_PAD - MLA_NOPE)))
    wk = k_plain.reshape(MLA_KV_RANK, -1)
    wvt = kvh[..., MLA_NOPE:].reshape(MLA_KV_RANK, -1).T
    poolw = jax.scipy.linalg.block_diag(*[pool_w[g] for g in range(N_SUB)])
    return win.astype(BF16), wq.astype(BF16), wk.astype(BF16), wvt.astype(BF16), poolw.astype(BF16)


def _pool_inv_counts(n):
    tok = jnp.arange(n, dtype=jnp.int32)[:, None]
    half = jnp.repeat(jnp.asarray(POOL_WINDOWS, jnp.int32) // 2, SUB_W)[None, :]
    cnt = jnp.minimum(tok + half - 1, n - 1) - jnp.maximum(tok - half, 0) + 1
    return 1.0 / cnt.astype(F32)


def _rope_tables(n):
    rows = n // GRID_W
    row = jnp.repeat(jnp.arange(rows, dtype=F32), GRID_W)
    col = jnp.tile(jnp.arange(GRID_W, dtype=F32), rows)
    half = MLA_ROPE // 2
    inv = ROPE_BASE ** (-jnp.arange(0, half, 2, dtype=F32) / half)
    ang_r = row[:, None] * inv[None, :]
    ang_c = col[:, None] * inv[None, :]
    ang = jnp.concatenate([ang_r, ang_r, ang_c, ang_c], axis=-1)
    return _place_tables(jnp.cos(ang), jnp.sin(ang))


def _place_tables(cos, sin):
    n = cos.shape[0]
    qs = MLA_SCALE * LOG2E
    pad_hi = HEAD_PAD - MLA_NOPE - MLA_ROPE
    zl, zh = jnp.zeros((n, MLA_NOPE), F32), jnp.zeros((n, pad_hi), F32)
    cosq = jnp.concatenate([jnp.full((n, MLA_NOPE), qs, F32), cos * qs, zh], axis=1)
    sinq = jnp.concatenate([zl, sin * qs, zh], axis=1)
    cosk = jnp.concatenate([zl, cos, zh], axis=1)
    sink = jnp.concatenate([zl, sin, zh], axis=1)
    return cosq, sinq, cosk, sink


def _tile(n, want):
    return min(n, want)


def kernel(x, c, ctx, c_ctx, ada_w, ada_b, norm1_g, norm2_g, w_in, fourier_w, conv_w, pool_w, pool_scale,
           q_norm_g, w_uq, kv_norm_g, w_ukv, w_out, mlp_w1, mlp_w2, final_norm_g):
    b, n, d = x.shape
    nc = ctx.shape[1]
    depth = ada_w.shape[0]

    rows = -(-(b + 1) // SUBLANES) * SUBLANES
    cc = jnp.concatenate([c, c_ctx[None, :], jnp.zeros((rows - b - 1, d), F32)], axis=0)
    mod = _ada(cc, ada_w, ada_b)
    cw = _fold_fourier(fourier_w)
    half_dft = n >= 2 * TOKEN_TILE
    g_x = _dft_matrix(n, n // 2 if half_dft else n)
    g_c = _dft_matrix(nc, nc)
    tabs_x = _rope_tables(n)
    tabs_c = _place_tables(jnp.ones((nc, MLA_ROPE), F32), jnp.zeros((nc, MLA_ROPE), F32))
    fg = final_norm_g.reshape(1, d)

    def stream_mod(l, ctx_stream):
        m = mod[l, b:b + 1] if ctx_stream else mod[l, :b]
        m = jnp.broadcast_to(m, (b, 6 * d)).reshape(b, 1, 6, d)
        return [m[:, :, j, :] for j in range(6)]

    for l in range(depth):
        last = l == depth - 1
        win, wq, wk, wvt, poolw = _layout_weights(w_in[l], w_uq[l], w_ukv[l], pool_w[l])
        n1g = norm1_g[l].reshape(1, d)
        n2g = norm2_g[l].reshape(1, d)
        qg = q_norm_g[l].reshape(1, -1)
        kvg = kv_norm_g[l].reshape(1, -1)
        pscale = pool_scale[l].reshape(1, -1)
        wout = w_out[l].astype(BF16)
        w1 = mlp_w1[l].astype(BF16)
        w2 = mlp_w2[l].astype(BF16)

        def project(tokens, ctx_stream, tabs):
            sh1, sc1 = stream_mod(l, ctx_stream)[:2]
            t = _tile(tokens.shape[1], PROJ_TILE)
            return _proj(tokens, sc1, sh1, n1g, win, cw[l], qg, wq, kvg, wk, wvt, *tabs, t)

        def finish(tokens, ctx_stream, cp, ab, alt, g, q, kv_sets, final):
            nt = tokens.shape[1]
            t = _tile(nt, TOKEN_TILE)
            _, _, g1, sh2, sc2, g2 = stream_mod(l, ctx_stream)
            cols = b * GROUP_W
            if g.shape[0] < nt:
                direct, mirrored = _dft_half(g, ab, _tile(nt // 2, DFT_TILE), _tile(cols, DFT_TILE),
                                             _tile(nt, DFT_WIDE))
                middle = jnp.sum(alt[:, :, 0, :], axis=1).reshape(1, cols) * (1.0 / math.sqrt(nt))
                f = (direct, mirrored, jnp.broadcast_to(middle, (SUBLANES, cols)).astype(BF16))
            else:
                f = _matmul(g, ab.reshape(2 * nt, cols), _tile(nt, DFT_TILE), _tile(cols, DFT_WIDE),
                            _tile(2 * nt, DFT_TILE))
            if t == 2 * ATTN_TILE:
                ats = _attn_pipelined(q, kv_sets, ATTN_TILE)
            else:
                ats = [[_attn(q, kv_sets, t).reshape(b * nt // t, t, GROUP_W)]]
            return _mixmlp(tokens, cp, f, ats, g1, sc2, sh2, g2, conv_w[l], poolw, pscale, wout, n2g, w1, w2,
                           fg, t, final)

        ab_c, cp_c, q_c, k_c, vt_c, alt_c = project(ctx, True, tabs_c)
        ab_x, cp_x, q_x, k_x, vt_x, alt_x = project(x, False, tabs_x)
        x = finish(x, False, cp_x, ab_x, alt_x, g_x, q_x, [(k_x, vt_x), (k_c, vt_c)], last)
        if not last:
            ctx = finish(ctx, True, cp_c, ab_c, alt_c, g_c, q_c, [(k_c, vt_c)], False)
    return x
```

```python
import functools
import math

import jax
import jax.numpy as jnp
from jax import lax
from jax.experimental import pallas as pl
from jax.experimental.pallas import tpu as pltpu

D_MODEL = 1024
DEPTH = 2
GRID_W = 64
GROUP_W = D_MODEL // 4
N_SUB = 4
SUB_W = GROUP_W // N_SUB
POOL_WINDOWS = (2, 4, 8, 16)
MLA_HEADS = 4
MLA_NOPE = GROUP_W // MLA_HEADS
MLA_ROPE = MLA_NOPE // 2
MLA_V = GROUP_W // MLA_HEADS
MLA_Q_RANK = GROUP_W
MLA_KV_RANK = GROUP_W // 2
MLA_SCALE = 1.0 / math.sqrt(MLA_NOPE + MLA_ROPE)
ROPE_BASE = 10000.0
D_FF = 4 * D_MODEL
EPS = 1e-6

OFF_CONV = GROUP_W
OFF_POOL = OFF_CONV + 3 * GROUP_W
OFF_MLA_Q = OFF_POOL + GROUP_W
OFF_MLA_KV = OFF_MLA_Q + MLA_Q_RANK
OFF_MLA_KR = OFF_MLA_KV + MLA_KV_RANK
IN_COLS = OFF_MLA_KR + MLA_ROPE

LANES = 128
SUBLANES = 8
HEAD_PAD = LANES
EXT_COLS = OFF_MLA_KR + LANES
DFT_ROWS = 64
HALO = SUBLANES
MLP_CHUNKS = 2
PROJ_SUB_ROWS = 512
VMEM_LIMIT = 52 * 1024 * 1024
ATTN_VMEM_LIMIT = 58 * 1024 * 1024
ATTN_SEG = 2048
ATTN_TILE = 256
TOKEN_TILE = 512
PROJ_TILE = 1024
ADA_COLS = 1536
DFT_TILE = 1024
DFT_WIDE = 2048
LOG2E = math.log2(math.e)

F32 = jnp.float32
BF16 = jnp.bfloat16


_NT = (((1,), (1,)), ((), ()))


def _dot(a, b):
    return jnp.dot(a, b, preferred_element_type=F32)


def _rms(x):
    return x * lax.rsqrt(jnp.mean(x * x, axis=-1, keepdims=True) + EPS)


def _params(*sem):
    return pltpu.CompilerParams(dimension_semantics=sem, vmem_limit_bytes=VMEM_LIMIT)


def _const_spec(shape):
    nd = len(shape)
    return pl.BlockSpec(shape, lambda *_: (0,) * nd, pipeline_mode=pl.Buffered(1))


def _ada_kernel(cc_ref, w_ref, b_ref, o_ref):
    cc = cc_ref[...]
    s = cc * (1.0 / (1.0 + jnp.exp(-cc)))
    o_ref[0] = _dot(s.astype(BF16), w_ref[0].astype(BF16)) + b_ref[0]


def _ada(cc, ada_w, ada_b):
    depth, d, cols = ada_w.shape
    rows = cc.shape[0]
    tn = ADA_COLS
    return pl.pallas_call(
        _ada_kernel,
        out_shape=jax.ShapeDtypeStruct((depth, rows, cols), F32),
        grid=(depth, cols // tn),
        in_specs=[pl.BlockSpec((rows, d), lambda l, j: (0, 0)),
                  pl.BlockSpec((1, d, tn), lambda l, j: (l, 0, j)),
                  pl.BlockSpec((1, 1, tn), lambda l, j: (l, 0, j))],
        out_specs=pl.BlockSpec((1, rows, tn), lambda l, j: (l, 0, j)),
        compiler_params=_params("arbitrary", "arbitrary"),
        name="ada",
    )(cc, ada_w, ada_b.reshape(depth, 1, cols))


def _fold_kernel(dft_ref, w_ref, o_ref):
    w = w_ref[0]
    c = jnp.dot(dft_ref[:, :GROUP_W], w, preferred_element_type=F32, precision=lax.Precision.HIGHEST)
    s = jnp.dot(dft_ref[:, GROUP_W:], w, preferred_element_type=F32, precision=lax.Precision.HIGHEST)
    o_ref[0, :, :GROUP_W] = c.astype(BF16)
    o_ref[0, :, GROUP_W:] = s.astype(BF16)


def _channel_dft():
    i = jnp.arange(GROUP_W, dtype=jnp.int32)
    same = (i[:, None] // SUB_W) == (i[None, :] // SUB_W)
    ang = ((i[:, None] % SUB_W) * (i[None, :] % SUB_W) % SUB_W).astype(F32) * (2.0 * math.pi / SUB_W)
    scale = 1.0 / math.sqrt(SUB_W)
    c = jnp.where(same, jnp.cos(ang), 0.0) * scale
    s = jnp.where(same, jnp.sin(ang), 0.0) * scale
    return jnp.concatenate([c, s], axis=1)


def _fold_fourier(fourier_w):
    depth = fourier_w.shape[0]
    return pl.pallas_call(
        _fold_kernel,
        out_shape=jax.ShapeDtypeStruct((depth, GROUP_W, 2 * GROUP_W), BF16),
        grid=(depth,),
        in_specs=[pl.BlockSpec((GROUP_W, 2 * GROUP_W), lambda l: (0, 0)),
                  pl.BlockSpec((1, GROUP_W, GROUP_W), lambda l: (l, 0, 0))],
        out_specs=pl.BlockSpec((1, GROUP_W, 2 * GROUP_W), lambda l: (l, 0, 0)),
        compiler_params=_params("arbitrary"),
        name="fold_fourier",
    )(_channel_dft(), fourier_w)


def _dftgen_kernel(n, c1_ref, s1_ref, cb_ref, sb_ref, o_ref):
    c1 = c1_ref[0]
    s1 = s1_ref[0]
    cb = cb_ref[...]
    sb = sb_ref[...]
    scale = 1.0 / math.sqrt(n)
    o_ref[:, :n] = ((c1 * cb - s1 * sb) * scale).astype(BF16)
    o_ref[:, n:] = ((s1 * cb + c1 * sb) * (-scale)).astype(BF16)


def _dft_matrix(n, rows):
    nb = rows // DFT_ROWS
    k = jnp.arange(n, dtype=jnp.int32)
    j1 = jnp.arange(nb, dtype=jnp.int32) * DFT_ROWS
    j0 = jnp.arange(DFT_ROWS, dtype=jnp.int32)
    w = 2.0 * math.pi / n
    a1 = ((j1[:, None] * k[None, :]) % n).astype(F32) * w
    a0 = ((j0[:, None] * k[None, :]) % n).astype(F32) * w
    row_spec = pl.BlockSpec((1, 1, n), lambda i: (i, 0, 0))
    tab_spec = pl.BlockSpec((DFT_ROWS, n), lambda i: (0, 0))
    return pl.pallas_call(
        functools.partial(_dftgen_kernel, n),
        out_shape=jax.ShapeDtypeStruct((rows, 2 * n), BF16),
        grid=(nb,),
        in_specs=[row_spec, row_spec, tab_spec, tab_spec],
        out_specs=pl.BlockSpec((DFT_ROWS, 2 * n), lambda i: (i, 0)),
        compiler_params=_params("arbitrary"),
        name="dft_matrix",
    )(jnp.cos(a1).reshape(nb, 1, n), jnp.sin(a1).reshape(nb, 1, n), jnp.cos(a0), jnp.sin(a0))


def _matmul_kernel(a_ref, b_ref, o_ref, acc_ref):
    k = pl.program_id(2)

    @pl.when(k == 0)
    def _():
        acc_ref[...] = jnp.zeros_like(acc_ref)

    acc_ref[...] += _dot(a_ref[...], b_ref[...])

    @pl.when(k == pl.num_programs(2) - 1)
    def _():
        o_ref[...] = acc_ref[...].astype(o_ref.dtype)


def _matmul(a, b, tm, tn, tk):
    m, kk = a.shape
    n = b.shape[1]
    return pl.pallas_call(
        _matmul_kernel,
        out_shape=jax.ShapeDtypeStruct((m, n), BF16),
        grid=(m // tm, n // tn, kk // tk),
        in_specs=[pl.BlockSpec((tm, tk), lambda i, j, k: (i, k)),
                  pl.BlockSpec((tk, tn), lambda i, j, k: (k, j))],
        out_specs=pl.BlockSpec((tm, tn), lambda i, j, k: (i, j)),
        scratch_shapes=[pltpu.VMEM((tm, tn), F32)],
        compiler_params=_params("parallel", "parallel", "arbitrary"),
        name="dft_matmul",
    )(a, b)


def _dft_half_kernel(c_ref, s_ref, a_ref, b_ref, direct_ref, mirrored_ref, accp_ref, accq_ref):
    k = pl.program_id(2)

    @pl.when(k == 0)
    def _():
        accp_ref[...] = jnp.zeros_like(accp_ref)
        accq_ref[...] = jnp.zeros_like(accq_ref)

    accp_ref[...] += _dot(c_ref[...], a_ref[0])
    accq_ref[...] += _dot(s_ref[...], b_ref[0])

    @pl.when(k == pl.num_programs(2) - 1)
    def _():
        direct_ref[...] = (accp_ref[...] + accq_ref[...]).astype(direct_ref.dtype)
        mirrored_ref[...] = (accp_ref[...] - accq_ref[...]).astype(mirrored_ref.dtype)


def _dft_half(g_top, ab, tm, tn, tk):
    h, n2 = g_top.shape
    n = n2 // 2
    cols = ab.shape[2]
    out = jax.ShapeDtypeStruct((h, cols), BF16)
    kb = n // tk
    return pl.pallas_call(
        _dft_half_kernel,
        out_shape=(out, out),
        grid=(h // tm, cols // tn, kb),
        in_specs=[pl.BlockSpec((tm, tk), lambda i, j, k: (i, k)),
                  pl.BlockSpec((tm, tk), lambda i, j, k: (i, k + kb)),
                  pl.BlockSpec((1, tk, tn), lambda i, j, k: (0, k, j)),
                  pl.BlockSpec((1, tk, tn), lambda i, j, k: (1, k, j))],
        out_specs=(pl.BlockSpec((tm, tn), lambda i, j, k: (i, j)),
                   pl.BlockSpec((tm, tn), lambda i, j, k: (i, j))),
        scratch_shapes=[pltpu.VMEM((tm, tn), F32), pltpu.VMEM((tm, tn), F32)],
        compiler_params=_params("parallel", "parallel", "arbitrary"),
        name="dft_half",
    )(g_top, g_top, ab, ab)


def _rope_rotate(v):
    quarter = MLA_ROPE // 4
    up = pltpu.roll(v, LANES - quarter, axis=1)
    down = pltpu.roll(v, quarter, axis=1)
    lane = lax.broadcasted_iota(jnp.int32, v.shape, 1)
    first = ((lane - MLA_NOPE) // quarter) % 2 == 0
    return jnp.where(first, -up, down)


def _proj_kernel(x_ref, sc_ref, sh_ref, g_ref, win_ref, cw_ref, qg_ref, wq_ref, kvg_ref, wk_ref, wvt_ref,
                 cosq_ref, sinq_ref, cosk_ref, sink_ref, ab_ref, cp_ref, q_ref, k_ref, vt_ref, alt_ref):
    gain = g_ref[...] * (1.0 + sc_ref[0])
    shift = sh_ref[0]
    t = x_ref.shape[1]
    sub = min(t, PROJ_SUB_ROWS)
    for r0 in range(0, t, sub):
        rows = slice(r0, r0 + sub)
        hb = (_rms(x_ref[0, rows]) * gain + shift).astype(BF16)

        cq = _dot(hb, win_ref[:, OFF_MLA_Q:OFF_MLA_KV])
        ckv = _dot(hb, win_ref[:, OFF_MLA_KV:OFF_MLA_KR])
        u = _dot(hb, win_ref[:, :OFF_CONV])
        kp1 = _dot(hb, win_ref[:, OFF_MLA_KR:EXT_COLS])
        cp_ref[0, rows] = _dot(hb, win_ref[:, OFF_CONV:OFF_MLA_Q])

        cqn = (_rms(cq) * qg_ref[...]).astype(BF16)
        qq = _dot(cqn, wq_ref[...])
        cosq = cosq_ref[rows]
        sinq = sinq_ref[rows]
        for hd in range(MLA_HEADS):
            lo = hd * HEAD_PAD
            qh = qq[:, lo:lo + HEAD_PAD]
            q_ref[0, rows, lo:lo + HEAD_PAD] = (qh * cosq + _rope_rotate(qh) * sinq).astype(BF16)

        ab = _dot(u.astype(BF16), cw_ref[...])
        ab_ref[0, rows] = ab[:, :GROUP_W].astype(BF16)
        ab_ref[1, rows] = ab[:, GROUP_W:].astype(BF16)
        even_row = lax.broadcasted_iota(jnp.int32, (sub, GROUP_W), 0) % 2 == 0
        alt_j = jnp.sum(jnp.where(even_row, ab[:, :GROUP_W], -ab[:, :GROUP_W]), axis=0, keepdims=True)
        alt = alt_j if r0 == 0 else alt + alt_j

        ckvn = (_rms(ckv) * kvg_ref[...]).astype(BF16)
        kn = _dot(ckvn, wk_ref[...])
        kp = kp1 * cosk_ref[rows] + _rope_rotate(kp1) * sink_ref[rows]
        for hd in range(MLA_HEADS):
            lo = hd * HEAD_PAD
            k_ref[0, rows, lo:lo + HEAD_PAD] = (kn[:, lo:lo + HEAD_PAD] + kp).astype(BF16)
        vt_ref[0, :, rows] = lax.dot_general(wvt_ref[...], ckvn, _NT, preferred_element_type=F32).astype(BF16)
    alt_ref[0, 0] = jnp.broadcast_to(alt, alt_ref.shape[2:])


def _proj(x, scale, shift, g, win, cw, qg, wq, kvg, wk, wvt, cosq, sinq, cosk, sink, t):
    b, n, d = x.shape
    row = lambda bi, i: (bi, 0, 0)
    tile = lambda bi, i: (bi, i, 0)
    tab = pl.BlockSpec((t, LANES), lambda bi, i: (i, 0))
    qk_w = MLA_HEADS * HEAD_PAD
    return pl.pallas_call(
        _proj_kernel,
        out_shape=(jax.ShapeDtypeStruct((2, n, b * GROUP_W), BF16),
                   jax.ShapeDtypeStruct((b, n, 4 * GROUP_W), F32),
                   jax.ShapeDtypeStruct((b, n, qk_w), BF16),
                   jax.ShapeDtypeStruct((b, n, qk_w), BF16),
                   jax.ShapeDtypeStruct((b, GROUP_W, n), BF16),
                   jax.ShapeDtypeStruct((b, n // t, SUBLANES, GROUP_W), F32)),
        grid=(b, n // t),
        in_specs=[pl.BlockSpec((1, t, d), tile),
                  pl.BlockSpec((1, 1, d), row),
                  pl.BlockSpec((1, 1, d), row),
                  _const_spec(g.shape), _const_spec(win.shape), _const_spec(cw.shape),
                  _const_spec(qg.shape), _const_spec(wq.shape), _const_spec(kvg.shape),
                  _const_spec(wk.shape), _const_spec(wvt.shape), tab, tab, tab, tab],
        out_specs=(pl.BlockSpec((2, t, GROUP_W), lambda bi, i: (0, i, bi)),
                   pl.BlockSpec((1, t, 4 * GROUP_W), tile),
                   pl.BlockSpec((1, t, qk_w), tile),
                   pl.BlockSpec((1, t, qk_w), tile),
                   pl.BlockSpec((1, GROUP_W, t), lambda bi, i: (bi, 0, i)),
                   pl.BlockSpec((1, 1, SUBLANES, GROUP_W), lambda bi, i: (bi, i, 0, 0))),
        compiler_params=_params("parallel", "parallel"),
        name="proj",
    )(x, scale, shift, g, win, cw, qg, wq, kvg, wk, wvt, cosq, sinq, cosk, sink)


def _key_segments(k_refs):
    segs = []
    off = 0
    for j, k_ref in enumerate(k_refs):
        nk = k_ref.shape[1]
        size = min(nk, ATTN_SEG)
        for start in range(0, nk, size):
            segs.append((j, start, size, off + start))
        off += nk
    return segs


def _seg_scores(q_ref, k_refs, hd, seg, s_ref):
    j, start, size, off = seg
    lo = hd * HEAD_PAD
    s = lax.dot_general(k_refs[j][0, start:start + size, lo:lo + HEAD_PAD], q_ref[0, :, lo:lo + HEAD_PAD], _NT,
                        preferred_element_type=F32)
    s_ref[hd, off:off + size] = s
    return jnp.max(s, axis=0, keepdims=True)


def _seg_softmax_pv(s_ref, m, vt_refs, hd, seg):
    j, start, size, off = seg
    p = jnp.exp2(s_ref[hd, off:off + size] - m)
    return (jnp.sum(p, axis=0, keepdims=True),
            _dot(vt_refs[j][0, hd * MLA_V:(hd + 1) * MLA_V, start:start + size], p.astype(BF16)))


def _head_pipeline(q_ref, k_refs, s_new, m_new, s_old, m_old, vt_refs, hd):
    m_prev = None if s_old is None else m_old[hd]
    m = l = o = None
    for seg in _key_segments(k_refs):
        if s_new is not None:
            mj = _seg_scores(q_ref, k_refs, hd, seg, s_new)
            m = mj if m is None else jnp.maximum(m, mj)
        if s_old is not None:
            lj, oj = _seg_softmax_pv(s_old, m_prev, vt_refs, hd, seg)
            l = lj if l is None else l + lj
            o = oj if o is None else o + oj
    if s_new is not None:
        m_new[hd] = m
    return None if s_old is None else o / l


def _store_heads(outs, o_ref):
    o_ref[0] = jnp.concatenate(outs, axis=0).T.astype(o_ref.dtype)


def _tile_scratch(n_keys, tq):
    return [pltpu.VMEM((MLA_HEADS, n_keys, tq), F32), pltpu.VMEM((MLA_HEADS, 1, tq), F32)]


def _attn_kernel(n_sets, q_ref, *refs):
    o_ref, s_ref, m_ref = refs[2 * n_sets:]
    k_refs, vt_refs = refs[:n_sets], refs[n_sets:2 * n_sets]
    for hd in range(MLA_HEADS):
        _head_pipeline(q_ref, k_refs, s_ref, m_ref, None, None, None, hd)
    _store_heads([_head_pipeline(None, k_refs, None, None, s_ref, m_ref, vt_refs, hd) for hd in range(MLA_HEADS)],
                 o_ref)


def _attn(q, kv_sets, tq):
    b, n, qk_w = q.shape
    whole = lambda bi, i: (bi, 0, 0)
    ks = [k for k, _ in kv_sets]
    vts = [vt for _, vt in kv_sets]
    return pl.pallas_call(
        functools.partial(_attn_kernel, len(kv_sets)),
        out_shape=jax.ShapeDtypeStruct((b, n, GROUP_W), BF16),
        grid=(b, n // tq),
        in_specs=([pl.BlockSpec((1, tq, qk_w), lambda bi, i: (bi, i, 0))]
                  + [pl.BlockSpec((1,) + k.shape[1:], whole) for k in ks]
                  + [pl.BlockSpec((1,) + vt.shape[1:], whole) for vt in vts]),
        out_specs=pl.BlockSpec((1, tq, GROUP_W), lambda bi, i: (bi, i, 0)),
        scratch_shapes=_tile_scratch(sum(k.shape[1] for k in ks), tq),
        compiler_params=_params("parallel", "arbitrary"),
        name="attn",
    )(q, *ks, *vts)


def _attn_pipe_kernel(n_sets, qe_ref, qo_ref, *refs):
    k_refs = refs[:n_sets]
    vt_prev = refs[n_sets:2 * n_sets]
    vt_cur = refs[2 * n_sets:3 * n_sets]
    oe_ref, oo_ref, s0, m0, s1, m1 = refs[3 * n_sets:]

    @pl.when(pl.program_id(0) == 0)
    def _():
        s1[...] = jnp.zeros_like(s1)
        m1[...] = jnp.zeros_like(m1)

    _store_heads([_head_pipeline(qe_ref, k_refs, s0, m0, s1, m1, vt_prev, hd) for hd in range(MLA_HEADS)], oo_ref)
    _store_heads([_head_pipeline(qo_ref, k_refs, s1, m1, s0, m0, vt_cur, hd) for hd in range(MLA_HEADS)], oe_ref)


def _attn_pipelined(q, kv_sets, tq):
    b, n, qk_w = q.shape
    nq = n // tq
    pairs = b * nq // 2
    ks = [k for k, _ in kv_sets]
    vts = [vt for _, vt in kv_sets]
    even = lambda j: jnp.minimum(2 * j, 2 * pairs - 2)
    b_cur = lambda j: even(j) // nq
    b_prev = lambda j: jnp.maximum(2 * j - 1, 0) // nq
    cur = lambda j: (jnp.minimum(j, pairs - 1), 0, 0)
    prev = lambda j: (jnp.maximum(j - 1, 0), 0, 0)
    out = jax.ShapeDtypeStruct((pairs, tq, GROUP_W), BF16)
    out_block = (1, tq, GROUP_W)
    n_keys = sum(k.shape[1] for k in ks)
    even_tiles, odd_tiles = pl.pallas_call(
        functools.partial(_attn_pipe_kernel, len(kv_sets)),
        out_shape=(out, out),
        grid=(pairs + 1,),
        in_specs=([pl.BlockSpec((1, tq, qk_w), lambda j: (b_cur(j), even(j) % nq, 0)),
                   pl.BlockSpec((1, tq, qk_w), lambda j: (b_cur(j), even(j) % nq + 1, 0))]
                  + [pl.BlockSpec((1,) + k.shape[1:], lambda j: (b_cur(j), 0, 0)) for k in ks]
                  + [pl.BlockSpec((1,) + vt.shape[1:], lambda j: (b_prev(j), 0, 0)) for vt in vts]
                  + [pl.BlockSpec((1,) + vt.shape[1:], lambda j: (b_cur(j), 0, 0)) for vt in vts]),
        out_specs=(pl.BlockSpec(out_block, cur), pl.BlockSpec(out_block, prev)),
        scratch_shapes=_tile_scratch(n_keys, tq) * 2,
        compiler_params=pltpu.CompilerParams(dimension_semantics=("arbitrary",),
                                             vmem_limit_bytes=ATTN_VMEM_LIMIT),
        name="attn_pipe",
    )(q, q, *ks, *vts, *vts)
    return [[even_tiles], [odd_tiles]]


def _fourier_tile(i, n_tiles, f_refs):
    if len(f_refs) == 1:
        return f_refs[0][...]
    direct_ref, mirrored_ref, mrow_ref, middle_ref, flip_ref = f_refs
    t = direct_ref.shape[0]
    flipped = _dot(flip_ref[...], mirrored_ref[...]).astype(BF16)
    row0 = jnp.where(2 * i == n_tiles, middle_ref[0:1, :], mrow_ref[0:1, :])
    is_row0 = lax.broadcasted_iota(jnp.int32, (t, GROUP_W), 0) == 0
    mirrored = jnp.where(is_row0, row0, flipped)
    return jnp.where(2 * i >= n_tiles, mirrored, direct_ref[...])


def _mixmlp_kernel(final, n_f, x_ref, cp_ref, prev_ref, next_ref, inv_ref, g1_ref, sc2_ref,
                   sh2_ref, g2_ref, convw_ref, poolw_ref, pscale_ref, wout_ref, n2g_ref, w1_ref, w2_ref,
                   fg_ref, *tail):
    f_refs, at_refs, o_ref = tail[:n_f], tail[n_f:-1], tail[-1]
    i = pl.program_id(1)
    t = x_ref.shape[1]
    first = i == 0
    last = i == pl.num_programs(1) - 1
    prev = jnp.where(first, 0.0, prev_ref[0])
    nxt = jnp.where(last, 0.0, next_ref[0])
    cpe = jnp.concatenate([prev, cp_ref[0], nxt], axis=0)
    ext = t + 2 * HALO
    inner = slice(HALO, HALO + t)
    lo_half = lax.broadcasted_iota(jnp.int32, (t, LANES), 1) < SUB_W
    per_row = GROUP_W // at_refs[0].shape[2]
    attn = jnp.concatenate([jnp.concatenate([r[0] for r in at_refs[k:k + per_row]], axis=-1)
                            for k in range(0, len(at_refs), per_row)], axis=0)
    fourier = _fourier_tile(i, pl.num_programs(1), f_refs)

    def shifted(a, k):
        return pltpu.roll(a, k % ext, axis=0)

    z = cpe[:, GROUP_W:2 * GROUP_W] * cpe[:, 2 * GROUP_W:3 * GROUP_W]
    y = shifted(z, 1) * convw_ref[0:1, :] + z * convw_ref[1:2, :] + shifted(z, -1) * convw_ref[2:3, :]
    conv = (cpe[:, :GROUP_W] * y)[inner]

    u = cpe[:, 3 * GROUP_W:]
    s2 = shifted(u, 1) + u
    s4 = shifted(s2, 1) + shifted(s2, -1)
    ub = u[:, LANES:]
    s4b = s4[:, LANES:]
    s8b = shifted(s4b, 2) + shifted(s4b, -2)
    s16b = shifted(s8b, 4) + shifted(s8b, -4)
    inv = inv_ref[...]
    pa = jnp.where(lo_half, s2[inner, :LANES], s4[inner, :LANES]) * inv[:, :LANES] - u[inner, :LANES]
    pb = jnp.where(lo_half, s8b[inner], s16b[inner]) * inv[:, LANES:] - ub[inner]
    pin = jnp.concatenate([pa, pb], axis=-1).astype(BF16)
    pool = _dot(pin, poolw_ref[...]) * pscale_ref[...]

    mix = jnp.concatenate([fourier, conv.astype(BF16), pool.astype(BF16), attn], axis=-1)
    x1 = x_ref[0] + g1_ref[0] * _dot(mix, wout_ref[...])

    h2 = (_rms(x1) * (n2g_ref[...] * (1.0 + sc2_ref[0])) + sh2_ref[0]).astype(BF16)
    chunk = D_FF // MLP_CHUNKS
    acc = None
    for c in range(MLP_CHUNKS):
        hid = jnp.maximum(_dot(h2, w1_ref[:, c * chunk:(c + 1) * chunk]), 0.0)
        part = _dot((hid * hid).astype(BF16), w2_ref[c * chunk:(c + 1) * chunk, :])
        acc = part if acc is None else acc + part
    x2 = x1 + g2_ref[0] * acc
    if final:
        x2 = _rms(x2) * fg_ref[...]
    o_ref[0] = x2


def _mixmlp(x, cp, f, ats, g1, sc2, sh2, g2, convw, poolw, pscale, wout, n2g, w1, w2, fg, t, final):
    b, n, d = x.shape
    if isinstance(f, tuple):
        direct, mirrored, middle = f
        nt, nh, hb8 = n // t, n // t // 2, t // SUBLANES
        r = jnp.arange(t, dtype=jnp.int32)
        flip = ((r[:, None] + r[None, :]) == t).astype(BF16)
        fs = [direct, mirrored, mirrored, middle, flip]
        f_specs = [pl.BlockSpec((t, GROUP_W), lambda bi, i: (jnp.minimum(i, nh - 1), bi)),
                   pl.BlockSpec((t, GROUP_W), lambda bi, i: (jnp.clip(nt - 1 - i, 0, nh - 1), bi)),
                   pl.BlockSpec((SUBLANES, GROUP_W), lambda bi, i: (jnp.clip((nt - i) * hb8, 0, nh * hb8 - 1), bi)),
                   pl.BlockSpec((SUBLANES, GROUP_W), lambda bi, i: (0, bi)),
                   _const_spec(flip.shape)]
    else:
        fs = [f]
        f_specs = [pl.BlockSpec((t, GROUP_W), lambda bi, i: (i, bi))]
    at_specs = [pl.BlockSpec((1,) + a.shape[1:], lambda bi, i: (bi * (n // t) + i, 0, 0)) for r in ats for a in r]
    ats = [a for r in ats for a in r]
    hb = t // HALO
    nhb = n // HALO
    row = lambda bi, i: (bi, 0, 0)
    tile = lambda bi, i: (bi, i, 0)
    row_spec = pl.BlockSpec((1, 1, d), row)
    return pl.pallas_call(
        functools.partial(_mixmlp_kernel, final, len(fs)),
        out_shape=jax.ShapeDtypeStruct((b, n, d), F32),
        grid=(b, n // t),
        in_specs=[pl.BlockSpec((1, t, d), tile),
                  pl.BlockSpec((1, t, 4 * GROUP_W), tile),
                  pl.BlockSpec((1, HALO, 4 * GROUP_W), lambda bi, i: (bi, jnp.maximum(i * hb - 1, 0), 0)),
                  pl.BlockSpec((1, HALO, 4 * GROUP_W), lambda bi, i: (bi, jnp.minimum((i + 1) * hb, nhb - 1), 0)),
                  pl.BlockSpec((t, GROUP_W), lambda bi, i: (i, 0)),
                  row_spec, row_spec, row_spec, row_spec,
                  _const_spec(convw.shape), _const_spec(poolw.shape), _const_spec(pscale.shape),
                  _const_spec(wout.shape), _const_spec(n2g.shape), _const_spec(w1.shape),
                  _const_spec(w2.shape), _const_spec(fg.shape)] + f_specs + at_specs,
        out_specs=pl.BlockSpec((1, t, d), tile),
        compiler_params=_params("parallel", "arbitrary"),
        name="mixmlp",
    )(x, cp, cp, cp, _pool_inv_counts(n), g1, sc2, sh2, g2, convw, poolw, pscale, wout, n2g, w1, w2, fg, *fs, *ats)


def _layout_weights(w_in, w_uq, w_ukv, pool_w):
    d = w_in.shape[0]
    kr = w_in[:, OFF_MLA_KR:]
    z = lambda r, c: jnp.zeros((r, c), F32)
    pad_lo, pad_hi = MLA_NOPE, HEAD_PAD - MLA_NOPE - MLA_ROPE
    win = jnp.concatenate([w_in[:, :OFF_MLA_KR], z(d, pad_lo), kr, z(d, pad_hi)], axis=1)
    qh = w_uq.reshape(MLA_Q_RANK, MLA_HEADS, MLA_NOPE + MLA_ROPE)
    wq = jnp.pad(qh, ((0, 0), (0, 0), (0, pad_hi))).reshape(MLA_Q_RANK, -1)
    kvh =w_ukv.reshape(MLA_KV_RANK, MLA_HEADS, MLA_NOPE + MLA_V)
    k_plain = jnp.pad(kvh[..., :MLA_NOPE], ((0, 0), (0, 0), (0, HEAD_PAD - MLA_NOPE)))
    wk = k_plain.reshape(MLA_KV_RANK, -1)
    wvt = kvh[..., MLA_NOPE:].reshape(MLA_KV_RANK, -1).T
    poolw = jax.scipy.linalg.block_diag(*[pool_w[g] for g in range(N_SUB)])
    return win.astype(BF16), wq.astype(BF16), wk.astype(BF16), wvt.astype(BF16), poolw.astype(BF16)


def _pool_inv_counts(n):
    tok = jnp.arange(n, dtype=jnp.int32)[:, None]
    half = jnp.repeat(jnp.asarray(POOL_WINDOWS, jnp.int32) // 2, SUB_W)[None, :]
    cnt = jnp.minimum(tok + half - 1, n - 1) - jnp.maximum(tok - half, 0) + 1
    return 1.0 / cnt.astype(F32)


def _rope_tables(n):
    rows = n // GRID_W
    row = jnp.repeat(jnp.arange(rows, dtype=F32), GRID_W)
    col = jnp.tile(jnp.arange(GRID_W, dtype=F32), rows)
    half = MLA_ROPE // 2
    inv = ROPE_BASE ** (-jnp.arange(0, half, 2, dtype=F32) / half)
    ang_r = row[:, None] * inv[None, :]
    ang_c = col[:, None] * inv[None, :]
    ang = jnp.concatenate([ang_r, ang_r, ang_c, ang_c], axis=-1)
    return _place_tables(jnp.cos(ang), jnp.sin(ang))


def _place_tables(cos, sin):
    n = cos.shape[0]
    qs = MLA_SCALE * LOG2E
    pad_hi = HEAD_PAD - MLA_NOPE - MLA_ROPE
    zl, zh = jnp.zeros((n, MLA_NOPE), F32), jnp.zeros((n, pad_hi), F32)
    cosq = jnp.concatenate([jnp.full((n, MLA_NOPE), qs, F32), cos * qs, zh], axis=1)
    sinq = jnp.concatenate([zl, sin * qs, zh], axis=1)
    cosk = jnp.concatenate([zl, cos, zh], axis=1)
    sink = jnp.concatenate([zl, sin, zh], axis=1)
    return cosq, sinq, cosk, sink


def _tile(n, want):
    return min(n, want)


def kernel(x, c, ctx, c_ctx, ada_w, ada_b, norm1_g, norm2_g, w_in, fourier_w, conv_w, pool_w, pool_scale,
           q_norm_g, w_uq, kv_norm_g, w_ukv, w_out, mlp_w1, mlp_w2, final_norm_g):
    b, n, d = x.shape
    nc = ctx.shape[1]
    depth = ada_w.shape[0]

    rows = -(-(b + 1) // SUBLANES) * SUBLANES
    cc = jnp.concatenate([c, c_ctx[None, :], jnp.zeros((rows - b - 1, d), F32)], axis=0)
    mod = _ada(cc, ada_w, ada_b)
    cw = _fold_fourier(fourier_w)
    half_dft = n >= 2 * TOKEN_TILE
    g_x = _dft_matrix(n, n // 2 if half_dft else n)
    g_c = _dft_matrix(nc, nc)
    tabs_x = _rope_tables(n)
    tabs_c = _place_tables(jnp.ones((nc, MLA_ROPE), F32), jnp.zeros((nc, MLA_ROPE), F32))
    fg = final_norm_g.reshape(1, d)

    def stream_mod(l, ctx_stream):
        m = mod[l, b:b + 1] if ctx_stream else mod[l, :b]
        m = jnp.broadcast_to(m, (b, 6 * d)).reshape(b, 1, 6, d)
        return [m[:, :, j, :] for j in range(6)]

    for l in range(depth):
        last = l == depth - 1
        win, wq, wk, wvt, poolw = _layout_weights(w_in[l], w_uq[l], w_ukv[l], pool_w[l])
        n1g = norm1_g[l].reshape(1, d)
        n2g = norm2_g[l].reshape(1, d)
        qg = q_norm_g[l].reshape(1, -1)
        kvg = kv_norm_g[l].reshape(1, -1)
        pscale = pool_scale[l].reshape(1, -1)
        wout = w_out[l].astype(BF16)
        w1 = mlp_w1[l].astype(BF16)
        w2 = mlp_w2[l].astype(BF16)

        def project(tokens, ctx_stream, tabs):
            sh1, sc1 = stream_mod(l, ctx_stream)[:2]
            t = _tile(tokens.shape[1], PROJ_TILE)
            return _proj(tokens, sc1, sh1, n1g, win, cw[l], qg, wq, kvg, wk, wvt, *tabs, t)

        def finish(tokens, ctx_stream, cp, ab, alt, g, q, kv_sets, final):
            nt = tokens.shape[1]
            t = _tile(nt, TOKEN_TILE)
            _, _, g1, sh2, sc2, g2 = stream_mod(l, ctx_stream)
            cols = b * GROUP_W
            if g.shape[0] < nt:
                direct, mirrored = _dft_half(g, ab, _tile(nt // 2, DFT_TILE), _tile(cols, DFT_TILE),
                                             _tile(nt, DFT_WIDE))
                middle = jnp.sum(alt[:, :, 0, :], axis=1).reshape(1, cols) * (1.0 / math.sqrt(nt))
                f = (direct, mirrored, jnp.broadcast_to(middle, (SUBLANES, cols)).astype(BF16))
            else:
                f = _matmul(g, ab.reshape(2 * nt, cols), _tile(nt, DFT_TILE), _tile(cols, DFT_WIDE),
                            _tile(2 * nt, DFT_TILE))
            if t == 2 * ATTN_TILE:
                ats = _attn_pipelined(q, kv_sets, ATTN_TILE)
            else:
                ats = [[_attn(q, kv_sets, t).reshape(b * nt // t, t, GROUP_W)]]
            return _mixmlp(tokens, cp, f, ats, g1, sc2, sh2, g2, conv_w[l], poolw, pscale, wout, n2g, w1, w2,
                           fg, t, final)

        ab_c, cp_c, q_c, k_c, vt_c, alt_c = project(ctx, True, tabs_c)
        ab_x, cp_x, q_x, k_x, vt_x, alt_x = project(x, False, tabs_x)
        x = finish(x, False, cp_x, ab_x, alt_x, g_x, q_x, [(k_x, vt_x), (k_c, vt_c)], last)
        if not last:
            ctx = finish(ctx, True, cp_c, ab_c, alt_c, g_c, q_c, [(k_c, vt_c)], False)
    return x
```
